```python
import math
import jax
import jax.numpy as jnp
from jax import lax
import numpy as np

D_MODEL = 4096
BATCH = 1
SEQ = 16384
DEPTH = 4

GRID_W = 64
HEAD_DIM = 128
N_HEADS = D_MODEL // HEAD_DIM
N_KV_HEADS = max(1, N_HEADS // 4)
KV_DIM = N_KV_HEADS * HEAD_DIM
NA_MAX_ROWS = 8
NA_KW = 16
Q_BLOCK = 128
ROPE_THETA = 10000.0
S5_GROUP = 16
S5_GROUPS = D_MODEL // S5_GROUP
S5_STATE = 64
S5_GROUPS_PER_BLOCK = math.gcd(32, S5_GROUPS)
S5_DT_MIN = 1e-3
S5_DT_MAX = 1e-1
D_FF = 4 * D_MODEL
PLE_DIM = 256
N_MIXERS = 3
N_NA_LAYERS = (DEPTH + 2) // 3
N_GQA_LAYERS = (DEPTH + 1) // 3
N_S5_LAYERS = DEPTH // 3
RES_SCALE = (2 * DEPTH) ** -0.5
EPS = 1e-6

kernel_name = 'hybrid_na_gqa_s5_encoder'


def rms_norm(x, gain):
    xf = x.astype(jnp.float32)
    y = xf * lax.rsqrt(jnp.mean(xf * xf, axis=-1, keepdims=True) + EPS)
    return (y * gain.astype(jnp.float32)).astype(x.dtype)


def neighbourhood_attention(hn, w_qkv, w_o, q_gain, k_gain, rpb):
    b, L, d = hn.shape
    rows = L // GRID_W
    kh = min(NA_MAX_ROWS, rows)
    kw = NA_KW
    q, k, v = jnp.split(hn @ w_qkv, 3, axis=-1)
    shp = (b, rows, GRID_W, N_HEADS, HEAD_DIM)
    q = rms_norm(q.reshape(shp), q_gain) * HEAD_DIM ** -0.5
    k = rms_norm(k.reshape(shp), k_gain)
    v = v.reshape(shp)
    cols = jnp.arange(GRID_W)
    c0 = jnp.clip(cols - kw // 2, 0, GRID_W - kw)
    col_idx = c0[:, None] + jnp.arange(kw)[None, :]
    col_rel = col_idx - cols[:, None] + (NA_KW - 1)

    def row_block(r):
        r0 = jnp.clip(r - kh // 2, 0, rows - kh)
        row_rel = jnp.arange(kh) - (r - r0) + (NA_MAX_ROWS - 1)
        k_win = lax.dynamic_slice_in_dim(k, r0, kh, axis=1)[:, :, col_idx]
        v_win = lax.dynamic_slice_in_dim(v, r0, kh, axis=1)[:, :, col_idx]
        q_r = lax.dynamic_index_in_dim(q, r, axis=1, keepdims=False)
        s = jnp.einsum('bwhd,bawkhd->bhwak', q_r, k_win).astype(jnp.float32)
        bias = rpb[:, row_rel[None, :, None], col_rel[:, None, :]]
        s = s + bias.astype(jnp.float32)[None]
        pr = jax.nn.softmax(s.reshape(b, N_HEADS, GRID_W, kh * kw), axis=-1).reshape(s.shape)
        return jnp.einsum('bhwak,bawkhd->bwhd', pr.astype(v.dtype), v_win)

    o = lax.map(row_block, jnp.arange(rows))
    o = jnp.moveaxis(o, 0, 1).reshape(b, L, d)
    return o @ w_o


def axial_rope_tables(L):
    t = jnp.arange(L)
    row = (t // GRID_W).astype(jnp.float32)
    col = (t % GRID_W).astype(jnp.float32)
    nf = HEAD_DIM // 4
    inv = 1.0 / (ROPE_THETA ** (jnp.arange(nf, dtype=jnp.float32) / nf))
    ang = jnp.stack([row[:, None] * inv, col[:, None] * inv], axis=1)
    return jnp.cos(ang), jnp.sin(ang)


def apply_axial_rope(x, cos, sin):
    b, L, h, dh = x.shape
    xf = x.astype(jnp.float32).reshape(b, L, h, 2, 2, dh // 4)
    x1, x2 = xf[..., 0, :], xf[..., 1, :]
    c = cos[None, :, None]
    s = sin[None, :, None]
    out = jnp.stack([x1 * c - x2 * s, x2 * c + x1 * s], axis=-2)
    return out.reshape(b, L, h, dh).astype(x.dtype)


def gqa_axial_attention(hn, w_qkv, w_o, q_gain, k_gain):
    b, L, d = hn.shape
    qkv = hn @ w_qkv
    q = qkv[..., :d].reshape(b, L, N_HEADS, HEAD_DIM)
    k = qkv[..., d:d + KV_DIM].reshape(b, L, N_KV_HEADS, HEAD_DIM)
    v = qkv[..., d + KV_DIM:].reshape(b, L, N_KV_HEADS, HEAD_DIM)
    cos, sin = axial_rope_tables(L)
    q = apply_axial_rope(rms_norm(q, q_gain), cos, sin) * HEAD_DIM ** -0.5
    k = apply_axial_rope(rms_norm(k, k_gain), cos, sin)
    groups = N_HEADS // N_KV_HEADS
    nb = L // Q_BLOCK
    qb = jnp.moveaxis(q.reshape(b, nb, Q_BLOCK, N_KV_HEADS, groups, HEAD_DIM), 1, 0)

    def block(qblk):
        s = jnp.einsum('bqkgd,bskd->bkgqs', qblk, k).astype(jnp.float32)
        pr = jax.nn.softmax(s, axis=-1)
        return jnp.einsum('bkgqs,bskd->bqkgd', pr.astype(v.dtype), v)

    o = lax.map(block, qb)
    o = jnp.moveaxis(o, 0, 1).reshape(b, L, d)
    return o @ w_o


def _complex_linear_combine(e1, e2):
    a1r, a1i, b1r, b1i = e1
    a2r, a2i, b2r, b2i = e2
    return (a2r * a1r - a2i * a1i,
            a2r * a1i + a2i * a1r,
            a2r * b1r - a2i * b1i + b2r,
            a2r * b1i + a2i * b1r + b2i)


def s5_scan_direction(u, lam_re, lam_im, log_step, b_re, b_im, c_re, c_im, reverse):
    dt = jnp.exp(log_step)[:, None]
    mag = jnp.exp(lam_re * dt)
    a_re = mag * jnp.cos(lam_im * dt)
    a_im = mag * jnp.sin(lam_im * dt)
    den = lam_re * lam_re + lam_im * lam_im
    num_re = a_re - 1.0
    z_re = (num_re * lam_re + a_im * lam_im) / den
    z_im = (a_im * lam_re - num_re * lam_im) / den
    bb_re = z_re[..., None] * b_re - z_im[..., None] * b_im
    bb_im = z_re[..., None] * b_im + z_im[..., None] * b_re
    bu_re = jnp.einsum('blgp,gnp->blgn', u, bb_re)
    bu_im = jnp.einsum('blgp,gnp->blgn', u, bb_im)
    ar = jnp.broadcast_to(a_re, bu_re.shape)
    ai = jnp.broadcast_to(a_im, bu_re.shape)
    _, _, x_re, x_im = lax.associative_scan(_complex_linear_combine, (ar, ai, bu_re, bu_im), reverse=reverse, axis=1)
    return jnp.einsum('blgn,gpn->blgp', x_re, c_re) - jnp.einsum('blgn,gpn->blgp', x_im, c_im)


def s5_mixer(hn, w_in, lam_re, lam_im, log_step, b_re, b_im, c_re, c_im, d_skip, w_out):
    bsz, L, d = hn.shape
    f32 = jnp.float32
    u = (hn @ w_in).astype(f32)
    nblk = S5_GROUPS // S5_GROUPS_PER_BLOCK
    gb = S5_GROUPS_PER_BLOCK
    u_blk = jnp.moveaxis(u.reshape(bsz, L, nblk, gb, S5_GROUP), 2, 0)

    def chunk_params(a):
        return jnp.moveaxis(a.astype(f32).reshape((2, nblk, gb) + a.shape[2:]), 1, 0)

    params = tuple(chunk_params(a) for a in (lam_re, lam_im, log_step, b_re, b_im, c_re, c_im))

    def group_block(args):
        u_c, lr, li, ls, br, bi, cr, ci = args
        y_fwd = s5_scan_direction(u_c, lr[0], li[0], ls[0], br[0], bi[0], cr[0], ci[0], False)
        y_bwd = s5_scan_direction(u_c, lr[1], li[1], ls[1], br[1], bi[1], cr[1], ci[1], True)
        return y_fwd + y_bwd

    y = lax.map(group_block, (u_blk,) + params)
    y = jnp.moveaxis(y, 0, 2).reshape(bsz, L, d)
    y = y + d_skip.astype(f32) * u
    g = jax.nn.gelu(y).astype(hn.dtype)
    a, gate = jnp.split(g @ w_out, 2, axis=-1)
    return a * jax.nn.sigmoid(gate)


def squared_relu_mlp(hn, w1, w2):
    return jnp.square(jax.nn.relu(hn @ w1)) @ w2


def setup_inputs(seed: int = 0) -> dict:
    key = jax.random.key(seed)
    k = jax.random.split(key, 28)
    f32 = jnp.float32
    D = D_MODEL

    def nrm(i, shape, std):
        return std * jax.random.normal(k[i], shape, f32)

    s5_shape = (N_S5_LAYERS, 2, S5_GROUPS, S5_STATE)
    n_idx = jnp.arange(S5_STATE, dtype=f32)
    return {
        'x': nrm(0, (BATCH, SEQ, D), 1.0),
        'p': nrm(1, (DEPTH, BATCH, SEQ, PLE_DIM), 1.0),
        'mix_norm': 1.0 + nrm(2, (DEPTH, D), 0.1),
        'mlp_norm': 1.0 + nrm(3, (DEPTH, D), 0.1),
        'ple_norm': 1.0 + nrm(4, (DEPTH, D), 0.1),
        'na_w_qkv': nrm(5, (N_NA_LAYERS, D, 3 * D), D ** -0.5),
        'na_w_o': nrm(6, (N_NA_LAYERS, D, D), RES_SCALE * D ** -0.5),
        'na_q_gain': 1.0 + nrm(7, (N_NA_LAYERS, HEAD_DIM), 0.1),
        'na_k_gain': 1.0 + nrm(8, (N_NA_LAYERS, HEAD_DIM), 0.1),
        'na_rpb': nrm(9, (N_NA_LAYERS, N_HEADS, 2 * NA_MAX_ROWS - 1, 2 * NA_KW - 1), 0.2),
        'gqa_w_qkv': nrm(10, (N_GQA_LAYERS, D, D + 2 * KV_DIM), D ** -0.5),
        'gqa_w_o': nrm(11, (N_GQA_LAYERS, D, D), RES_SCALE * D ** -0.5),
        'gqa_q_gain': 1.0 + nrm(12, (N_GQA_LAYERS, HEAD_DIM), 0.1),
        'gqa_k_gain': 1.0 + nrm(13, (N_GQA_LAYERS, HEAD_DIM), 0.1),
        's5_w_in': nrm(14, (N_S5_LAYERS, D, D), D ** -0.5),
        's5_lam_re': -0.5 + nrm(15, s5_shape, 0.05),
        's5_lam_im': math.pi * n_idx + nrm(16, s5_shape, 0.05),
        's5_log_step': jax.random.uniform(k[17], (N_S5_LAYERS, 2, S5_GROUPS), f32, math.log(S5_DT_MIN), math.log(S5_DT_MAX)),
        's5_b_re': nrm(18, s5_shape + (S5_GROUP,), (2 * S5_GROUP) ** -0.5),
        's5_b_im': nrm(19, s5_shape + (S5_GROUP,), (2 * S5_GROUP) ** -0.5),
        's5_c_re': nrm(20, (N_S5_LAYERS, 2, S5_GROUPS, S5_GROUP, S5_STATE), S5_STATE ** -0.5),
        's5_c_im': nrm(21, (N_S5_LAYERS, 2, S5_GROUPS, S5_GROUP, S5_STATE), S5_STATE ** -0.5),
        's5_d': nrm(22, (N_S5_LAYERS, D), 1.0),
        's5_w_out': nrm(23, (N_S5_LAYERS, D, 2 * D), RES_SCALE * D ** -0.5),
        'mlp_w1': nrm(24, (DEPTH, D, D_FF), D ** -0.5),
        'mlp_w2': nrm(25, (DEPTH, D_FF, D), RES_SCALE * D_FF ** -0.5),
        'ple_w_proj': nrm(26, (DEPTH, PLE_DIM, D), RES_SCALE * PLE_DIM ** -0.5),
        'ple_w_gate': nrm(27, (DEPTH, D, D), D ** -0.5),
    }


def reference(x, p, mix_norm, mlp_norm, ple_norm, na_w_qkv, na_w_o, na_q_gain, na_k_gain, na_rpb,
              gqa_w_qkv, gqa_w_o, gqa_q_gain, gqa_k_gain, s5_w_in, s5_lam_re, s5_lam_im, s5_log_step,
              s5_b_re, s5_b_im, s5_c_re, s5_c_im, s5_d, s5_w_out, mlp_w1, mlp_w2, ple_w_proj, ple_w_gate):
    h = x
    for i in range(DEPTH):
        kind = i % N_MIXERS
        slot = i // N_MIXERS
        hn = rms_norm(h, mix_norm[i])
        if kind == 0:
            y = neighbourhood_attention(hn, na_w_qkv[slot], na_w_o[slot], na_q_gain[slot], na_k_gain[slot], na_rpb[slot])
        elif kind == 1:
            y = gqa_axial_attention(hn, gqa_w_qkv[slot], gqa_w_o[slot], gqa_q_gain[slot], gqa_k_gain[slot])
        else:
            y = s5_mixer(hn, s5_w_in[slot], s5_lam_re[slot], s5_lam_im[slot], s5_log_step[slot],
                         s5_b_re[slot], s5_b_im[slot], s5_c_re[slot], s5_c_im[slot], s5_d[slot], s5_w_out[slot])
        h = h + y.astype(h.dtype)
        h = h + squared_relu_mlp(rms_norm(h, mlp_norm[i]), mlp_w1[i], mlp_w2[i]).astype(h.dtype)
        gate = jax.nn.sigmoid(rms_norm(h, ple_norm[i]) @ ple_w_gate[i])
        h = h + (gate * (p[i].astype(h.dtype) @ ple_w_proj[i])).astype(h.dtype)
    return h
```

```python
import functools
import math

import jax
import jax.numpy as jnp
from jax import lax
from jax.experimental import pallas as pl
from jax.experimental.pallas import tpu as pltpu

GRID_W = 64
HEAD_DIM = 128
NA_MAX_ROWS = 8
NA_KW = 16
NA_ROWS_PER_BLOCK = 4
NA_WIN_BLOCKS = 3
ROPE_THETA = 10000.0
S5_GROUP = 16
S5_STATE = 64
S5_GROUPS_PER_STEP = 16
S5_SEGMENTS = 8
EPS = 1e-6
MASK_VALUE = -1e30
V7X_VMEM_LIMIT_BYTES = 56 * 1024 * 1024

F32 = jnp.float32
BF16 = jnp.bfloat16


def _params(*sem):
    return pltpu.CompilerParams(dimension_semantics=sem, vmem_limit_bytes=V7X_VMEM_LIMIT_BYTES)


def _tile(n, want):
    t = min(n, want)
    while n % t:
        t //= 2
    return t


def _rmsnorm_kernel(x_ref, g_ref, o_ref):
    x = x_ref[...]
    ms = jnp.mean(x * x, axis=-1, keepdims=True)
    o_ref[...] = (x * lax.rsqrt(ms + EPS) * g_ref[...]).astype(o_ref.dtype)


def rmsnorm(x, gain, *, tm=512):
    m, d = x.shape
    tm = _tile(m, tm)
    return pl.pallas_call(
        _rmsnorm_kernel,
        out_shape=jax.ShapeDtypeStruct((m, d), BF16),
        grid=(m // tm,),
        in_specs=[pl.BlockSpec((tm, d), lambda i: (i, 0)), pl.BlockSpec((1, d), lambda i: (0, 0))],
        out_specs=pl.BlockSpec((tm, d), lambda i: (i, 0)),
        compiler_params=_params("parallel"),
        name="rmsnorm",
    )(x, gain.reshape(1, d).astype(F32))


def _dot(a, b):
    return jnp.dot(a, b, preferred_element_type=F32)


def _mm_kernel(a_ref, w_ref, o_ref, *, relu2):
    acc = _dot(a_ref[...], w_ref[...])
    if relu2:
        acc = jnp.square(jnp.maximum(acc, 0.0))
    o_ref[...] = acc.astype(o_ref.dtype)


def matmul(a, w, *, out_dtype, relu2=False, tm=1024, tn=512):
    m, k = a.shape
    n = w.shape[1]
    tm, tn = _tile(m, tm), _tile(n, tn)
    return pl.pallas_call(
        functools.partial(_mm_kernel, relu2=relu2),
        out_shape=jax.ShapeDtypeStruct((m, n), out_dtype),
        grid=(m // tm, n // tn),
        in_specs=[pl.BlockSpec((tm, k), lambda i, j: (i, 0)), pl.BlockSpec((k, tn), lambda i, j: (0, j))],
        out_specs=pl.BlockSpec((tm, tn), lambda i, j: (i, j)),
        compiler_params=_params("parallel", "parallel"),
        name="matmul",
    )(a, w)


def _mm_res_kernel(a_ref, w_ref, r_ref, o_ref):
    @pl.when(pl.program_id(2) == 0)
    def _():
        o_ref[...] = r_ref[...]

    o_ref[...] += _dot(a_ref[...], w_ref[...])


def matmul_residual(a, w, res, *, tm=1024, tn=1024, tk=2048):
    m, k = a.shape
    n = w.shape[1]
    tm, tn, tk = _tile(m, tm), _tile(n, tn), _tile(k, tk)
    return pl.pallas_call(
        _mm_res_kernel,
        out_shape=jax.ShapeDtypeStruct((m, n), F32),
        grid=(m // tm, n // tn, k // tk),
        in_specs=[
            pl.BlockSpec((tm, tk), lambda i, j, kk: (i, kk)),
            pl.BlockSpec((tk, tn), lambda i, j, kk: (kk, j)),
            pl.BlockSpec((tm, tn), lambda i, j, kk: (i, j)),
        ],
        out_specs=pl.BlockSpec((tm, tn), lambda i, j, kk: (i, j)),
        compiler_params=_params("parallel", "parallel", "arbitrary"),
        name="matmul_residual",
    )(a, w, res)


def _glu_res_kernel(a_ref, wa_ref, wg_ref, r_ref, o_ref):
    a = a_ref[...]
    val = _dot(a, wa_ref[...])
    gate = _dot(a, wg_ref[...])
    o_ref[...] = r_ref[...] + val * jax.nn.sigmoid(gate)


def glu_residual(a, w, res, *, tm=512, tn=512):
    m, k = a.shape
    n = w.shape[1] // 2
    tm, tn = _tile(m, tm), _tile(n, tn)
    nj = n // tn
    return pl.pallas_call(
        _glu_res_kernel,
        out_shape=jax.ShapeDtypeStruct((m, n), F32),
        grid=(m // tm, nj),
        in_specs=[
            pl.BlockSpec((tm, k), lambda i, j: (i, 0)),
            pl.BlockSpec((k, tn), lambda i, j: (0, j)),
            pl.BlockSpec((k, tn), lambda i, j: (0, j + nj)),
            pl.BlockSpec((tm, tn), lambda i, j: (i, j)),
        ],
        out_specs=pl.BlockSpec((tm, tn), lambda i, j: (i, j)),
        compiler_params=_params("parallel", "parallel"),
        name="glu_residual",
    )(a, w, w, res)


def _ple_kernel(a_ref, wg_ref, p_ref, wp_ref, r_ref, o_ref):
    gate = jax.nn.sigmoid(_dot(a_ref[...], wg_ref[...]))
    proj = _dot(p_ref[...], wp_ref[...])
    o_ref[...] = r_ref[...] + gate * proj


def ple_residual(a, w_gate, p, w_proj, res, *, tm=1024, tn=512):
    m, k = a.shape
    n = w_gate.shape[1]
    kp = p.shape[1]
    tm, tn = _tile(m, tm), _tile(n, tn)
    return pl.pallas_call(
        _ple_kernel,
        out_shape=jax.ShapeDtypeStruct((m, n), F32),
        grid=(m // tm, n // tn),
        in_specs=[
            pl.BlockSpec((tm, k), lambda i, j: (i, 0)),
            pl.BlockSpec((k, tn), lambda i, j: (0, j)),
            pl.BlockSpec((tm, kp), lambda i, j: (i, 0)),
            pl.BlockSpec((kp, tn), lambda i, j: (0, j)),
            pl.BlockSpec((tm, tn), lambda i, j: (i, j)),
        ],
        out_specs=pl.BlockSpec((tm, tn), lambda i, j: (i, j)),
        compiler_params=_params("parallel", "parallel"),
        name="ple_residual",
    )(a, w_gate, p, w_proj, res)


def _head_norm(x, gain):
    x = x.astype(F32)
    return x * lax.rsqrt(jnp.mean(x * x, axis=-1, keepdims=True) + EPS) * gain


def _na_kernel(q_ref, k0_ref, k1_ref, k2_ref, v0_ref, v1_ref, v2_ref, bias_ref, qg_ref, kg_ref, o_ref):
    q = (_head_norm(q_ref[...], qg_ref[...]) * HEAD_DIM ** -0.5).astype(BF16)
    k = jnp.concatenate([k0_ref[...], k1_ref[...], k2_ref[...]], axis=0)
    k = _head_norm(k, kg_ref[...]).astype(BF16)
    v = jnp.concatenate([v0_ref[...], v1_ref[...], v2_ref[...]], axis=0)
    s = lax.dot_general(q, k, (((1,), (1,)), ((), ())), preferred_element_type=F32)
    s = s + bias_ref[...]
    m = jnp.max(s, axis=-1, keepdims=True)
    p = jnp.exp(s - m)
    l = jnp.sum(p, axis=-1, keepdims=True)
    o = _dot(p.astype(BF16), v)
    o_ref[...] = (o / l).astype(o_ref.dtype)


def _na_bias_table(rpb, rows):
    rb, kh, kw, w = NA_ROWS_PER_BLOCK, NA_MAX_ROWS, NA_KW, GRID_W
    win = rb * NA_WIN_BLOCKS
    assert rows >= win and rows % rb == 0 and rows // rb >= 3
    i = jnp.arange(rb)
    dq = jnp.stack([i, rb + i, 2 * rb + i])
    d0 = jnp.stack([jnp.zeros_like(i), i, jnp.full_like(i, win - kh)])
    a = jnp.arange(win)
    row_ok = (a[None, None, :] >= d0[:, :, None]) & (a[None, None, :] < d0[:, :, None] + kh)
    row_rel = jnp.clip(a[None, None, :] - dq[:, :, None] + (kh - 1), 0, 2 * kh - 2)
    cols = jnp.arange(w)
    c0 = jnp.clip(cols - kw // 2, 0, w - kw)
    col_ok = (cols[None, :] >= c0[:, None]) & (cols[None, :] < c0[:, None] + kw)
    col_rel = jnp.clip(cols[None, :] - cols[:, None] + (kw - 1), 0, 2 * kw - 2)
    ok = row_ok[:, :, None, :, None] & col_ok[None, None, :, None, :]
    row_sel = jax.nn.one_hot(row_rel, 2 * kh - 1, dtype=F32)
    col_sel = jax.nn.one_hot(col_rel, 2 * kw - 1, dtype=F32)
    bias = jnp.einsum("tiar,hrc,wkc->thiwak", row_sel, rpb.astype(F32), col_sel,
                      precision=lax.Precision.HIGHEST)
    bias = jnp.where(ok[:, None], bias, MASK_VALUE)
    h = rpb.shape[0]
    return bias.reshape(3, h, rb * w, win * w)


def na_attention(qkv, q_gain, k_gain, rpb):
    seq, d3 = qkv.shape
    d = d3 // 3
    nh = d // HEAD_DIM
    rows = seq // GRID_W
    qb = NA_ROWS_PER_BLOCK * GRID_W
    nb = seq // qb
    bias = _na_bias_table(rpb, rows)
    last_start = nb - NA_WIN_BLOCKS

    def kv_spec(part, off):
        return pl.BlockSpec(
            (qb, HEAD_DIM), lambda h, b: (jnp.clip(b - 1, 0, last_start) + off, part * nh + h))

    def bias_map(h, b):
        return ((b > 0).astype(jnp.int32) + (b == nb - 1).astype(jnp.int32), h, 0, 0)

    gain_spec = pl.BlockSpec((1, HEAD_DIM), lambda h, b: (0, 0))
    return pl.pallas_call(
        _na_kernel,
        out_shape=jax.ShapeDtypeStruct((seq, d), BF16),
        grid=(nh, nb),
        in_specs=[pl.BlockSpec((qb, HEAD_DIM), lambda h, b: (b, h))]
        + [kv_spec(1, o) for o in range(NA_WIN_BLOCKS)]
        + [kv_spec(2, o) for o in range(NA_WIN_BLOCKS)]
        + [pl.BlockSpec((None, None, qb, NA_WIN_BLOCKS * qb), bias_map), gain_spec, gain_spec],
        out_specs=pl.BlockSpec((qb, HEAD_DIM), lambda h, b: (b, h)),
        compiler_params=_params("parallel", "arbitrary"),
        name="na_attention",
    )(qkv, qkv, qkv, qkv, qkv, qkv, qkv, bias,
      q_gain.reshape(1, HEAD_DIM).astype(F32), k_gain.reshape(1, HEAD_DIM).astype(F32))


def _rope_tables(seq):
    t = jnp.arange(seq)
    row = (t // GRID_W).astype(F32)
    col = (t % GRID_W).astype(F32)
    nf = HEAD_DIM // 4
    inv = 1.0 / (ROPE_THETA ** (jnp.arange(nf, dtype=F32) / nf))
    ar, ac = row[:, None] * inv, col[:, None] * inv
    cos = jnp.concatenate([jnp.cos(ar), jnp.cos(ar), jnp.cos(ac), jnp.cos(ac)], axis=1)
    sin = jnp.concatenate([-jnp.sin(ar), jnp.sin(ar), -jnp.sin(ac), jnp.sin(ac)], axis=1)
    return cos, sin


def _qk_prep_kernel(x_ref, g_ref, s_ref, cos_ref, sin_ref, o_ref):
    x = _head_norm(x_ref[...], g_ref[...])
    nf = HEAD_DIM // 4
    lane = lax.broadcasted_iota(jnp.int32, x.shape, 1)
    partner = jnp.where((lane % (2 * nf)) < nf,
                        pltpu.roll(x, HEAD_DIM - nf, axis=1), pltpu.roll(x, nf, axis=1))
    y = x * cos_ref[...] + partner * sin_ref[...]
    o_ref[...] = (y * s_ref[...]).astype(o_ref.dtype)


def qk_prep(qkv, gains, scales, n_qk_heads, *, tl=512):
    seq = qkv.shape[0]
    tl = _tile(seq, tl)
    cos, sin = _rope_tables(seq)
    head_spec = pl.BlockSpec((None, 1, HEAD_DIM), lambda i, j: (j, 0, 0))
    tab_spec = pl.BlockSpec((tl, HEAD_DIM), lambda i, j: (i, 0))
    return pl.pallas_call(
        _qk_prep_kernel,
        out_shape=jax.ShapeDtypeStruct((seq, n_qk_heads * HEAD_DIM), BF16),
        grid=(seq // tl, n_qk_heads),
        in_specs=[pl.BlockSpec((tl, HEAD_DIM), lambda i, j: (i, j)), head_spec, head_spec, tab_spec, tab_spec],
        out_specs=pl.BlockSpec((tl, HEAD_DIM), lambda i, j: (i, j)),
        compiler_params=_params("parallel", "arbitrary"),
        name="qk_prep",
    )(qkv, gains, scales, cos, sin)


def _flash_kernel(q_ref, k_ref, v_ref, o_ref, qs_ref, m_ref, l_ref, acc_ref, *, groups, tq, tk):
    for g in range(groups):
        qs_ref[g * tq:(g + 1) * tq, :] = q_ref[:, g * HEAD_DIM:(g + 1) * HEAD_DIM]
    m_ref[...] = jnp.full(m_ref.shape, -jnp.inf, F32)
    l_ref[...] = jnp.zeros(l_ref.shape, F32)
    acc_ref[...] = jnp.zeros(acc_ref.shape, F32)
    q = qs_ref[...]
    nk = k_ref.shape[0] // tk

    def body(c, carry):
        off = pl.multiple_of(c * tk, tk)
        k = k_ref[pl.ds(off, tk), :]
        v = v_ref[pl.ds(off, tk), :]
        s = lax.dot_general(q, k, (((1,), (1,)), ((), ())), preferred_element_type=F32)
        m_old = m_ref[...]
        m_new = jnp.maximum(m_old, jnp.max(s, axis=-1, keepdims=True))
        alpha = jnp.exp(m_old - m_new)
        p = jnp.exp(s - m_new)
        l_ref[...] = alpha * l_ref[...] + jnp.sum(p, axis=-1, keepdims=True)
        acc_ref[...] = alpha * acc_ref[...] + _dot(p.astype(BF16), v)
        m_ref[...] = m_new
        return carry

    lax.fori_loop(0, nk, body, 0)
    out = acc_ref[...] / l_ref[...]
    for g in range(groups):
        o_ref[:, g * HEAD_DIM:(g + 1) * HEAD_DIM] = out[g * tq:(g + 1) * tq, :].astype(o_ref.dtype)


def flash_gqa(qk, qkv, n_heads, n_kv, *, tq=256, tk=512):
    seq = qk.shape[0]
    groups = n_heads // n_kv
    tq, tk = _tile(seq, tq), _tile(seq, tk)
    gw = groups * HEAD_DIM
    v_col0 = n_heads + n_kv
    return pl.pallas_call(
        functools.partial(_flash_kernel, groups=groups, tq=tq, tk=tk),
        out_shape=jax.ShapeDtypeStruct((seq, n_heads * HEAD_DIM), BF16),
        grid=(n_kv, seq // tq),
        in_specs=[
            pl.BlockSpec((tq, gw), lambda g, i: (i, g)),
            pl.BlockSpec((seq, HEAD_DIM), lambda g, i: (0, n_heads + g)),
            pl.BlockSpec((seq, HEAD_DIM), lambda g, i: (0, v_col0 + g)),
        ],
        out_specs=pl.BlockSpec((tq, gw), lambda g, i: (i, g)),
        scratch_shapes=[
            pltpu.VMEM((groups * tq, HEAD_DIM), BF16),
            pltpu.VMEM((groups * tq, 1), F32),
            pltpu.VMEM((groups * tq, 1), F32),
            pltpu.VMEM((groups * tq, HEAD_DIM), F32),
        ],
        compiler_params=_params("parallel", "arbitrary"),
        name="flash_gqa",
    )(qk, qk, qkv)


def _s5_discretize_kernel(lr_ref, li_ref, ls_ref, br_ref, bi_ref, ar_ref, ai_ref, bbr_ref, bbi_ref):
    lam_re, lam_im = lr_ref[...], li_ref[...]
    dt = jnp.exp(ls_ref[...])
    mag = jnp.exp(lam_re * dt)
    a_re = mag * jnp.cos(lam_im * dt)
    a_im = mag * jnp.sin(lam_im * dt)
    den = lam_re * lam_re + lam_im * lam_im
    num_re = a_re - 1.0
    z_re = (num_re * lam_re + a_im * lam_im) / den
    z_im = (a_im * lam_re - num_re * lam_im) / den
    ar_ref[...] = a_re
    ai_ref[...] = a_im
    bbr_ref[...] = z_re * br_ref[...] - z_im * bi_ref[...]
    bbi_ref[...] = z_re * bi_ref[...] + z_im * br_ref[...]


def s5_discretize(lam_re, lam_im, log_step, b_re, b_im):
    two, g, n = lam_re.shape
    p = b_re.shape[-1]
    cols = two * g * n
    flat = lambda a: a.astype(F32).reshape(1, cols)
    ls = jnp.broadcast_to(log_step.astype(F32)[:, :, None], (two, g, n)).reshape(1, cols)
    bt = lambda b: jnp.moveaxis(b.astype(F32), 3, 0).reshape(p, cols)
    tc = _tile(cols, 4096)
    row = pl.BlockSpec((1, tc), lambda i: (0, i))
    mat = pl.BlockSpec((p, tc), lambda i: (0, i))
    a_re, a_im, bb_re, bb_im = pl.pallas_call(
        _s5_discretize_kernel,
        out_shape=[jax.ShapeDtypeStruct((1, cols), F32)] * 2 + [jax.ShapeDtypeStruct((p, cols), F32)] * 2,
        grid=(cols // tc,),
        in_specs=[row, row, row, mat, mat],
        out_specs=[row, row, mat, mat],
        compiler_params=_params("parallel"),
        name="s5_discretize",
    )(flat(lam_re), flat(lam_im), ls, bt(b_re), bt(b_im))
    shp = (two, g, n)
    return a_re.reshape(shp), a_im.reshape(shp), bb_re.reshape((p,) + shp), bb_im.reshape((p,) + shp)


def _s5_scan_kernel(*refs, n_dir_blocks, kc, seg_len, full):
    if full:
        u_ref, bmat_ref, a_ref, ends_ref, cmat_ref, y_ref, state_ref, bu_ref, x_ref = refs
    else:
        u_ref, bmat_ref, a_ref, ends_ref, state_ref, bu_ref = refs
    nseg = S5_SEGMENTS
    lanes = bu_ref.shape[2]
    nt = bu_ref.shape[0]
    half = nt * lanes // 2
    dg = pl.program_id(0)
    c = pl.program_id(1)
    rev = dg >= n_dir_blocks
    a_re = a_ref[0:1, :]
    a_im = a_ref[1:2, :]

    @pl.when(c == 0)
    def _():
        if not full:
            state_ref[...] = jnp.zeros(state_ref.shape, F32)
        else:
            pr, pi = a_re, a_im
            for _ in range(int(math.log2(seg_len))):
                pr, pi = pr * pr - pi * pi, 2.0 * pr * pi
            e = ends_ref[...]
            zero = jnp.zeros((1, half), F32)
            fwd = [(zero, zero)]
            for s in range(1, nseg):
                cr, ci = fwd[-1]
                fwd.append((e[s - 1:s, :half] + pr * cr - pi * ci, e[s - 1:s, half:] + pr * ci + pi * cr))
            bwd = [(zero, zero)]
            for s in range(nseg - 2, -1, -1):
                cr, ci = bwd[-1]
                bwd.append((e[s + 1:s + 2, :half] + pr * cr - pi * ci, e[s + 1:s + 2, half:] + pr * ci + pi * cr))
            bwd = bwd[::-1]
            for s in range(nseg):
                state_ref[s:s + 1, :half] = jnp.where(rev, bwd[s][0], fwd[s][0])
                state_ref[s:s + 1, half:] = jnp.where(rev, bwd[s][1], fwd[s][1])

    u = u_ref[...].reshape(nseg * kc, u_ref.shape[2]).astype(BF16)
    bu = _dot(u, bmat_ref[...])
    for j in range(nt):
        bu_ref[j] = bu[:, j * lanes:(j + 1) * lanes]

    ar = jnp.broadcast_to(a_re, (nseg, half))
    ai = jnp.broadcast_to(a_im, (nseg, half))

    def body(i, carry):
        xr, xi = carry
        k = jnp.where(rev, kc - 1 - i, i)
        rows = pl.ds(k, nseg, stride=kc)
        br = jnp.concatenate([bu_ref[j, rows, :] for j in range(nt // 2)], axis=1)
        bi = jnp.concatenate([bu_ref[j, rows, :] for j in range(nt // 2, nt)], axis=1)
        nr = ar * xr - ai * xi + br
        ni = ar * xi + ai * xr + bi
        if full:
            for j in range(nt // 2):
                x_ref[j, rows, :] = nr[:, j * lanes:(j + 1) * lanes]
                x_ref[nt // 2 + j, rows, :] = ni[:, j * lanes:(j + 1) * lanes]
        return nr, ni

    xr, xi = lax.fori_loop(0, kc, body, (state_ref[:, :half], state_ref[:, half:]))
    state_ref[:, :half] = xr
    state_ref[:, half:] = xi

    if full:
        x = jnp.concatenate([x_ref[j] for j in range(nt)], axis=1)
        y = _dot(x.astype(BF16), cmat_ref[...])
        y_ref[...] = y.reshape(y_ref.shape)
    else:
        @pl.when(c == pl.num_programs(1) - 1)
        def _():
            ends_ref[...] = state_ref[...]


def _s5_scan_call(u3, bmat, a2, ends, cmat, *, kc, full):
    nseg, seg_len, d = u3.shape
    ndg, cw, sw = bmat.shape
    nblk = ndg // 2
    nc = seg_len // kc

    def chunk(dg, c):
        return jnp.where(dg >= nblk, nc - 1 - c, c)

    in_specs = [
        pl.BlockSpec((nseg, kc, cw), lambda dg, c: (0, chunk(dg, c), dg % nblk)),
        pl.BlockSpec((None, cw, sw), lambda dg, c: (dg, 0, 0)),
        pl.BlockSpec((None, 2, sw // 2), lambda dg, c: (dg, 0, 0)),
    ]
    ends_spec = pl.BlockSpec((None, nseg, sw), lambda dg, c: (dg, 0, 0))
    lane_tiled = pltpu.VMEM((sw // 128, nseg * kc, 128), F32)
    scratch = [pltpu.VMEM((nseg, sw), F32), lane_tiled]
    kern = functools.partial(_s5_scan_kernel, n_dir_blocks=nblk, kc=kc, seg_len=seg_len, full=full)
    if full:
        return pl.pallas_call(
            kern,
            out_shape=jax.ShapeDtypeStruct((2, nseg, seg_len, d), F32),
            grid=(ndg, nc),
            in_specs=in_specs + [ends_spec, pl.BlockSpec((None, sw, cw), lambda dg, c: (dg, 0, 0))],
            out_specs=pl.BlockSpec((None, nseg, kc, cw), lambda dg, c: (dg // nblk, 0, chunk(dg, c), dg % nblk)),
            scratch_shapes=scratch + [lane_tiled],
            compiler_params=_params("parallel", "arbitrary"),
            name="s5_scan",
        )(u3, bmat, a2, ends, cmat)
    return pl.pallas_call(
        kern,
        out_shape=jax.ShapeDtypeStruct((ndg, nseg, sw), F32),
        grid=(ndg, nc),
        in_specs=in_specs,
        out_specs=ends_spec,
        scratch_shapes=scratch,
        compiler_params=_params("parallel", "arbitrary"),
        name="s5_segment_ends",
    )(u3, bmat, a2)


def _block_diag(m, gps):
    two, g, r, c = m.shape
    m = m.reshape(two, g // gps, gps, r, c)
    eye = jnp.eye(gps, dtype=m.dtype)
    bd = m[:, :, :, :, None, :] * eye[None, None, :, None, :, None]
    return bd.reshape(two * (g // gps), gps * r, gps * c)


def s5_scan(u, lam_re, lam_im, log_step, b_re, b_im, c_re, c_im, *, kc=64):
    seq, d = u.shape
    g = d // S5_GROUP
    gps = min(S5_GROUPS_PER_STEP, g)
    nseg = S5_SEGMENTS
    seg_len = seq // nseg
    assert seq % nseg == 0 and seg_len & (seg_len - 1) == 0
    kc = _tile(seg_len, kc)
    a_re, a_im, bb_re, bb_im = s5_discretize(lam_re, lam_im, log_step, b_re, b_im)
    nblk = g // gps
    ndg = 2 * nblk
    a2 = jnp.stack([a_re.reshape(ndg, gps * S5_STATE), a_im.reshape(ndg, gps * S5_STATE)], axis=1)
    to_gpn = lambda b: jnp.transpose(b, (1, 2, 0, 3))
    bmat = jnp.concatenate([_block_diag(to_gpn(bb_re), gps), _block_diag(to_gpn(bb_im), gps)],
                           axis=2).astype(BF16)
    to_gnp = lambda cc: jnp.swapaxes(cc.astype(F32), 2, 3)
    cmat = jnp.concatenate([_block_diag(to_gnp(c_re), gps), -_block_diag(to_gnp(c_im), gps)],
                           axis=1).astype(BF16)
    u3 = u.reshape(nseg, seg_len, d)
    ends = _s5_scan_call(u3, bmat, a2, None, None, kc=kc, full=False)
    y = _s5_scan_call(u3, bmat, a2, ends, cmat, kc=kc, full=True)
    return y.reshape(2, seq, d)


def _s5_gelu_kernel(y_ref, u_ref, d_ref, o_ref):
    y = y_ref[0] + y_ref[1] + d_ref[...] * u_ref[...]
    o_ref[...] = jax.nn.gelu(y).astype(o_ref.dtype)


def s5_gelu(y2, u, d_skip, *, tm=256):
    seq, d = u.shape
    tm = _tile(seq, tm)
    return pl.pallas_call(
        _s5_gelu_kernel,
        out_shape=jax.ShapeDtypeStruct((seq, d), BF16),
        grid=(seq // tm,),
        in_specs=[pl.BlockSpec((2, tm, d), lambda i: (0, i, 0)), pl.BlockSpec((tm, d), lambda i: (i, 0)),
                  pl.BlockSpec((1, d), lambda i: (0, 0))],
        out_specs=pl.BlockSpec((tm, d), lambda i: (i, 0)),
        compiler_params=_params("parallel"),
        name="s5_gelu",
    )(y2, u, d_skip.reshape(1, d).astype(F32))


def na_mixer(h, hn, w_qkv, w_o, q_gain, k_gain, rpb):
    qkv = matmul(hn, w_qkv.astype(BF16), out_dtype=BF16)
    o = na_attention(qkv, q_gain, k_gain, rpb)
    return matmul_residual(o, w_o.astype(BF16), h, tn=512, tk=w_o.shape[0])


def gqa_mixer(h, hn, w_qkv, w_o, q_gain, k_gain):
    d = h.shape[1]
    n_heads = d // HEAD_DIM
    n_kv = max(1, n_heads // 4)
    qkv = matmul(hn, w_qkv.astype(BF16), out_dtype=BF16)
    gains = jnp.concatenate([jnp.broadcast_to(q_gain.astype(F32), (n_heads, HEAD_DIM)),
                             jnp.broadcast_to(k_gain.astype(F32), (n_kv, HEAD_DIM))]).reshape(-1, 1, HEAD_DIM)
    scales = jnp.concatenate([jnp.full((n_heads, HEAD_DIM), HEAD_DIM ** -0.5, F32),
                              jnp.ones((n_kv, HEAD_DIM), F32)]).reshape(-1, 1, HEAD_DIM)
    qk = qk_prep(qkv, gains, scales, n_heads + n_kv)
    o = flash_gqa(qk, qkv, n_heads, n_kv)
    return matmul_residual(o, w_o.astype(BF16), h, tn=512, tk=w_o.shape[0])


def s5_mixer(h, hn, w_in, lam_re, lam_im, log_step, b_re, b_im, c_re, c_im, d_skip, w_out):
    u = matmul(hn, w_in.astype(BF16), out_dtype=F32)
    y2 = s5_scan(u, lam_re, lam_im, log_step, b_re, b_im, c_re, c_im)
    g = s5_gelu(y2, u, d_skip)
    return glu_residual(g, w_out.astype(BF16), h)


def kernel(x, p, mix_norm, mlp_norm, ple_norm, na_w_qkv, na_w_o, na_q_gain, na_k_gain, na_rpb, gqa_w_qkv, gqa_w_o, gqa_q_gain, gqa_k_gain, s5_w_in, s5_lam_re, s5_lam_im, s5_log_step, s5_b_re, s5_b_im, s5_c_re, s5_c_im, s5_d, s5_w_out, mlp_w1, mlp_w2, ple_w_proj, ple_w_gate):
    b, seq, d = x.shape
    depth = mix_norm.shape[0]
    outs = []
    for bi in range(b):
        h = x[bi]
        for i in range(depth):
            kind, slot = i % 3, i // 3
            hn = rmsnorm(h, mix_norm[i])
            if kind == 0:
                h = na_mixer(h, hn, na_w_qkv[slot], na_w_o[slot], na_q_gain[slot], na_k_gain[slot], na_rpb[slot])
            elif kind == 1:
                h = gqa_mixer(h, hn, gqa_w_qkv[slot], gqa_w_o[slot], gqa_q_gain[slot], gqa_k_gain[slot])
            else:
                h = s5_mixer(h, hn, s5_w_in[slot], s5_lam_re[slot], s5_lam_im[slot], s5_log_step[slot],
                             s5_b_re[slot], s5_b_im[slot], s5_c_re[slot], s5_c_im[slot], s5_d[slot], s5_w_out[slot])
            hn = rmsnorm(h, mlp_norm[i])
            a = matmul(hn, mlp_w1[i].astype(BF16), out_dtype=BF16, relu2=True)
            h = matmul_residual(a, mlp_w2[i].astype(BF16), h)
            hn = rmsnorm(h, ple_norm[i])
            h = ple_residual(hn, ple_w_gate[i].astype(BF16), p[i, bi].astype(BF16), ple_w_proj[i].astype(BF16), h)
        outs.append(h)
    return jnp.stack(outs)
```

```python
import functools
import math

import jax
import jax.numpy as jnp
from jax import lax
from jax.experimental import pallas as pl
from jax.experimental.pallas import tpu as pltpu

GRID_W = 64
HEAD_DIM = 128
NA_MAX_ROWS = 8
NA_KW = 16
NA_ROWS_PER_BLOCK = 4
NA_WIN_BLOCKS = 3
ROPE_THETA = 10000.0
S5_GROUP = 16
S5_STATE = 64
S5_GROUPS_PER_STEP = 16
S5_SEGMENTS = 8
S5_SCAN_UNROLL = 8
EPS = 1e-6
MASK_VALUE = -1e30
V7X_VMEM_LIMIT_BYTES = 56 * 1024 * 1024

F32 = jnp.float32
BF16 = jnp.bfloat16


def _params(*sem):
    return pltpu.CompilerParams(dimension_semantics=sem, vmem_limit_bytes=V7X_VMEM_LIMIT_BYTES)


def _tile(n, want):
    t = min(n, want)
    while n % t:
        t //= 2
    return t


def _rmsnorm_kernel(x_ref, g_ref, o_ref):
    x = x_ref[...]
    ms = jnp.mean(x * x, axis=-1, keepdims=True)
    o_ref[...] = (x * lax.rsqrt(ms + EPS) * g_ref[...]).astype(o_ref.dtype)


def rmsnorm(x, gain, *, tm=512):
    m, d = x.shape
    tm = _tile(m, tm)
    return pl.pallas_call(
        _rmsnorm_kernel,
        out_shape=jax.ShapeDtypeStruct((m, d), BF16),
        grid=(m // tm,),
        in_specs=[pl.BlockSpec((tm, d), lambda i: (i, 0)), pl.BlockSpec((1, d), lambda i: (0, 0))],
        out_specs=pl.BlockSpec((tm, d), lambda i: (i, 0)),
        compiler_params=_params("parallel"),
        name="rmsnorm",
    )(x, gain.reshape(1, d).astype(F32))


def _dot(a, b):
    return jnp.dot(a, b, preferred_element_type=F32)


def _mm_kernel(a_ref, w_ref, o_ref, *, relu2):
    acc = _dot(a_ref[...], w_ref[...])
    if relu2:
        acc = jnp.square(jnp.maximum(acc, 0.0))
    o_ref[...] = acc.astype(o_ref.dtype)


def matmul(a, w, *, out_dtype, relu2=False, tm=1024, tn=512):
    m, k = a.shape
    n = w.shape[1]
    tm, tn = _tile(m, tm), _tile(n, tn)
    return pl.pallas_call(
        functools.partial(_mm_kernel, relu2=relu2),
        out_shape=jax.ShapeDtypeStruct((m, n), out_dtype),
        grid=(m // tm, n // tn),
        in_specs=[pl.BlockSpec((tm, k), lambda i, j: (i, 0)), pl.BlockSpec((k, tn), lambda i, j: (0, j))],
        out_specs=pl.BlockSpec((tm, tn), lambda i, j: (i, j)),
        compiler_params=_params("parallel", "parallel"),
        name="matmul",
    )(a, w)


def _mm_res_kernel(a_ref, w_ref, r_ref, o_ref):
    @pl.when(pl.program_id(2) == 0)
    def _():
        o_ref[...] = r_ref[...]

    o_ref[...] += _dot(a_ref[...], w_ref[...])


def matmul_residual(a, w, res, *, tm=1024, tn=1024, tk=2048):
    m, k = a.shape
    n = w.shape[1]
    tm, tn, tk = _tile(m, tm), _tile(n, tn), _tile(k, tk)
    return pl.pallas_call(
        _mm_res_kernel,
        out_shape=jax.ShapeDtypeStruct((m, n), F32),
        grid=(m // tm, n // tn, k // tk),
        in_specs=[
            pl.BlockSpec((tm, tk), lambda i, j, kk: (i, kk)),
            pl.BlockSpec((tk, tn), lambda i, j, kk: (kk, j)),
            pl.BlockSpec((tm, tn), lambda i, j, kk: (i, j)),
        ],
        out_specs=pl.BlockSpec((tm, tn), lambda i, j, kk: (i, j)),
        compiler_params=_params("parallel", "parallel", "arbitrary"),
        name="matmul_residual",
    )(a, w, res)


def _glu_res_kernel(a_ref, wa_ref, wg_ref, r_ref, o_ref):
    a = a_ref[...]
    val = _dot(a, wa_ref[...])
    gate = _dot(a, wg_ref[...])
    o_ref[...] = r_ref[...] + val * jax.nn.sigmoid(gate)


def glu_residual(a, w, res, *, tm=512, tn=512):
    m, k = a.shape
    n = w.shape[1] // 2
    tm, tn = _tile(m, tm), _tile(n, tn)
    nj = n // tn
    return pl.pallas_call(
        _glu_res_kernel,
        out_shape=jax.ShapeDtypeStruct((m, n), F32),
        grid=(m // tm, nj),
        in_specs=[
            pl.BlockSpec((tm, k), lambda i, j: (i, 0)),
            pl.BlockSpec((k, tn), lambda i, j: (0, j)),
            pl.BlockSpec((k, tn), lambda i, j: (0, j + nj)),
            pl.BlockSpec((tm, tn), lambda i, j: (i, j)),
        ],
        out_specs=pl.BlockSpec((tm, tn), lambda i, j: (i, j)),
        compiler_params=_params("parallel", "parallel"),
        name="glu_residual",
    )(a, w, w, res)


def _ple_kernel(a_ref, wg_ref, p_ref, wp_ref, r_ref, o_ref):
    gate = jax.nn.sigmoid(_dot(a_ref[...], wg_ref[...]))
    proj = _dot(p_ref[...], wp_ref[...])
    o_ref[...] = r_ref[...] + gate * proj


def ple_residual(a, w_gate, p, w_proj, res, *, tm=1024, tn=512):
    m, k = a.shape
    n = w_gate.shape[1]
    kp = p.shape[1]
    tm, tn = _tile(m, tm), _tile(n, tn)
    return pl.pallas_call(
        _ple_kernel,
        out_shape=jax.ShapeDtypeStruct((m, n), F32),
        grid=(m // tm, n // tn),
        in_specs=[
            pl.BlockSpec((tm, k), lambda i, j: (i, 0)),
            pl.BlockSpec((k, tn), lambda i, j: (0, j)),
            pl.BlockSpec((tm, kp), lambda i, j: (i, 0)),
            pl.BlockSpec((kp, tn), lambda i, j: (0, j)),
            pl.BlockSpec((tm, tn), lambda i, j: (i, j)),
        ],
        out_specs=pl.BlockSpec((tm, tn), lambda i, j: (i, j)),
        compiler_params=_params("parallel", "parallel"),
        name="ple_residual",
    )(a, w_gate, p, w_proj, res)


def _head_norm(x, gain):
    x = x.astype(F32)
    return x * lax.rsqrt(jnp.mean(x * x, axis=-1, keepdims=True) + EPS) * gain


def _na_kernel(q_ref, k0_ref, k1_ref, k2_ref, v0_ref, v1_ref, v2_ref, bias_ref, qg_ref, kg_ref, o_ref):
    q = (_head_norm(q_ref[...], qg_ref[...]) * HEAD_DIM ** -0.5).astype(BF16)
    k = jnp.concatenate([k0_ref[...], k1_ref[...], k2_ref[...]], axis=0)
    k = _head_norm(k, kg_ref[...]).astype(BF16)
    v = jnp.concatenate([v0_ref[...], v1_ref[...], v2_ref[...]], axis=0)
    s = lax.dot_general(q, k, (((1,), (1,)), ((), ())), preferred_element_type=F32)
    s = s + bias_ref[...]
    m = jnp.max(s, axis=-1, keepdims=True)
    p = jnp.exp(s - m)
    l = jnp.sum(p, axis=-1, keepdims=True)
    o = _dot(p.astype(BF16), v)
    o_ref[...] = (o / l).astype(o_ref.dtype)


def _na_bias_table(rpb, rows):
    rb, kh, kw, w = NA_ROWS_PER_BLOCK, NA_MAX_ROWS, NA_KW, GRID_W
    win = rb * NA_WIN_BLOCKS
    assert rows >= win and rows % rb == 0 and rows // rb >= 3
    i = jnp.arange(rb)
    dq = jnp.stack([i, rb + i, 2 * rb + i])
    d0 = jnp.stack([jnp.zeros_like(i), i, jnp.full_like(i, win - kh)])
    a = jnp.arange(win)
    row_ok = (a[None, None, :] >= d0[:, :, None]) & (a[None, None, :] < d0[:, :, None] + kh)
    row_rel = jnp.clip(a[None, None, :] - dq[:, :, None] + (kh - 1), 0, 2 * kh - 2)
    cols = jnp.arange(w)
    c0 = jnp.clip(cols - kw // 2, 0, w - kw)
    col_ok = (cols[None, :] >= c0[:, None]) & (cols[None, :] < c0[:, None] + kw)
    col_rel = jnp.clip(cols[None, :] - cols[:, None] + (kw - 1), 0, 2 * kw - 2)
    ok = row_ok[:, :, None, :, None] & col_ok[None, None, :, None, :]
    row_sel = jax.nn.one_hot(row_rel, 2 * kh - 1, dtype=F32)
    col_sel = jax.nn.one_hot(col_rel, 2 * kw - 1, dtype=F32)
    bias = jnp.einsum("tiar,hrc,wkc->thiwak", row_sel, rpb.astype(F32), col_sel,
                      precision=lax.Precision.HIGHEST)
    bias = jnp.where(ok[:, None], bias, MASK_VALUE)
    h = rpb.shape[0]
    return bias.reshape(3, h, rb * w, win * w)


def na_attention(qkv, q_gain, k_gain, rpb):
    seq, d3 = qkv.shape
    d = d3 // 3
    nh = d // HEAD_DIM
    rows = seq // GRID_W
    qb = NA_ROWS_PER_BLOCK * GRID_W
    nb = seq // qb
    bias = _na_bias_table(rpb, rows)
    last_start = nb - NA_WIN_BLOCKS

    def kv_spec(part, off):
        return pl.BlockSpec(
            (qb, HEAD_DIM), lambda h, b: (jnp.clip(b - 1, 0, last_start) + off, part * nh + h))

    def bias_map(h, b):
        return ((b > 0).astype(jnp.int32) + (b == nb - 1).astype(jnp.int32), h, 0, 0)

    gain_spec = pl.BlockSpec((1, HEAD_DIM), lambda h, b: (0, 0))
    return pl.pallas_call(
        _na_kernel,
        out_shape=jax.ShapeDtypeStruct((seq, d), BF16),
        grid=(nh, nb),
        in_specs=[pl.BlockSpec((qb, HEAD_DIM), lambda h, b: (b, h))]
        + [kv_spec(1, o) for o in range(NA_WIN_BLOCKS)]
        + [kv_spec(2, o) for o in range(NA_WIN_BLOCKS)]
        + [pl.BlockSpec((None, None, qb, NA_WIN_BLOCKS * qb), bias_map), gain_spec, gain_spec],
        out_specs=pl.BlockSpec((qb, HEAD_DIM), lambda h, b: (b, h)),
        compiler_params=_params("parallel", "arbitrary"),
        name="na_attention",
    )(qkv, qkv, qkv, qkv, qkv, qkv, qkv, bias,
      q_gain.reshape(1, HEAD_DIM).astype(F32), k_gain.reshape(1, HEAD_DIM).astype(F32))


def _rope_tables(seq):
    t = jnp.arange(seq)
    row = (t // GRID_W).astype(F32)
    col = (t % GRID_W).astype(F32)
    nf = HEAD_DIM // 4
    inv = 1.0 / (ROPE_THETA ** (jnp.arange(nf, dtype=F32) / nf))
    ar, ac = row[:, None] * inv, col[:, None] * inv
    cos = jnp.concatenate([jnp.cos(ar), jnp.cos(ar), jnp.cos(ac), jnp.cos(ac)], axis=1)
    sin = jnp.concatenate([-jnp.sin(ar), jnp.sin(ar), -jnp.sin(ac), jnp.sin(ac)], axis=1)
    return cos, sin


def _qk_prep_kernel(x_ref, g_ref, s_ref, cos_ref, sin_ref, o_ref):
    x = _head_norm(x_ref[...], g_ref[...])
    nf = HEAD_DIM // 4
    lane = lax.broadcasted_iota(jnp.int32, x.shape, 1)
    partner = jnp.where((lane % (2 * nf)) < nf,
                        pltpu.roll(x, HEAD_DIM - nf, axis=1), pltpu.roll(x, nf, axis=1))
    y = x * cos_ref[...] + partner * sin_ref[...]
    o_ref[...] = (y * s_ref[...]).astype(o_ref.dtype)


def qk_prep(qkv, gains, scales, n_qk_heads, *, tl=512):
    seq = qkv.shape[0]
    tl = _tile(seq, tl)
    cos, sin = _rope_tables(seq)
    head_spec = pl.BlockSpec((None, 1, HEAD_DIM), lambda i, j: (j, 0, 0))
    tab_spec = pl.BlockSpec((tl, HEAD_DIM), lambda i, j: (i, 0))
    return pl.pallas_call(
        _qk_prep_kernel,
        out_shape=jax.ShapeDtypeStruct((seq, n_qk_heads * HEAD_DIM), BF16),
        grid=(seq // tl, n_qk_heads),
        in_specs=[pl.BlockSpec((tl, HEAD_DIM), lambda i, j: (i, j)), head_spec, head_spec, tab_spec, tab_spec],
        out_specs=pl.BlockSpec((tl, HEAD_DIM), lambda i, j: (i, j)),
        compiler_params=_params("parallel", "arbitrary"),
        name="qk_prep",
    )(qkv, gains, scales, cos, sin)


FLASH_ONES_ROWS = 16


def _flash_kernel(q_ref, k_ref, vt_ref, o_ref, qs_ref, m_ref, acc_ref, st_ref, *, groups, tq, tk):
    for g in range(groups):
        qs_ref[g * tq:(g + 1) * tq, :] = q_ref[:, g * HEAD_DIM:(g + 1) * HEAD_DIM]
    m_ref[...] = jnp.full(m_ref.shape, -jnp.inf, F32)
    acc_ref[...] = jnp.zeros(acc_ref.shape, F32)
    q = qs_ref[...]
    nk = k_ref.shape[0] // tk

    def scores(c):
        k = k_ref[pl.ds(pl.multiple_of(c * tk, tk), tk), :]
        st = lax.dot_general(k, q, (((1,), (1,)), ((), ())), preferred_element_type=F32)
        return st.astype(BF16)

    def accumulate(st, c):
        vt = vt_ref[:, pl.ds(pl.multiple_of(c * tk, tk), tk)]
        m_old = m_ref[...]
        m_new = jnp.maximum(m_old, jnp.max(st, axis=0, keepdims=True).astype(F32))
        alpha = jnp.exp2(m_old - m_new)
        p = jnp.exp2(st - m_new.astype(BF16))
        acc_ref[...] = alpha * acc_ref[...] + _dot(vt, p)
        m_ref[...] = m_new

    st_ref[0] = scores(0)

    def body(i, carry):
        c = 2 * i
        st_ref[1] = scores(c + 1)
        accumulate(st_ref[0], c)
        st_ref[0] = scores(jnp.minimum(c + 2, nk - 1))
        accumulate(st_ref[1], c + 1)
        return carry

    lax.fori_loop(0, nk // 2, body, 0)
    acc = acc_ref[...]
    out = (acc[:HEAD_DIM] / acc[HEAD_DIM:HEAD_DIM + 1]).T
    for g in range(groups):
        o_ref[:, g * HEAD_DIM:(g + 1) * HEAD_DIM] = out[g * tq:(g + 1) * tq, :].astype(o_ref.dtype)


def flash_gqa(qk, vt, n_heads, n_kv, *, tq=256, tk=512):
    seq = qk.shape[0]
    groups = n_heads // n_kv
    tq, tk = _tile(seq, tq), _tile(seq, tk)
    gw = groups * HEAD_DIM
    vrows = vt.shape[1]
    assert (seq // tk) % 2 == 0
    return pl.pallas_call(
        functools.partial(_flash_kernel, groups=groups, tq=tq, tk=tk),
        out_shape=jax.ShapeDtypeStruct((seq, n_heads * HEAD_DIM), BF16),
        grid=(n_kv, seq // tq),
        in_specs=[
            pl.BlockSpec((tq, gw), lambda g, i: (i, g)),
            pl.BlockSpec((seq, HEAD_DIM), lambda g, i: (0, n_heads + g)),
            pl.BlockSpec((None, vrows, seq), lambda g, i: (g, 0, 0)),
        ],
        out_specs=pl.BlockSpec((tq, gw), lambda g, i: (i, g)),
        scratch_shapes=[
            pltpu.VMEM((groups * tq, HEAD_DIM), BF16),
            pltpu.VMEM((1, groups * tq), F32),
            pltpu.VMEM((vrows, groups * tq), F32),
            pltpu.VMEM((2, tk, groups * tq), BF16),
        ],
        compiler_params=_params("parallel", "arbitrary"),
        name="flash_gqa",
    )(qk, qk, vt)


def _s5_discretize_kernel(lr_ref, li_ref, ls_ref, br_ref, bi_ref, ar_ref, ai_ref, bbr_ref, bbi_ref):
    lam_re, lam_im = lr_ref[...], li_ref[...]
    dt = jnp.exp(ls_ref[...])
    mag = jnp.exp(lam_re * dt)
    a_re = mag * jnp.cos(lam_im * dt)
    a_im = mag * jnp.sin(lam_im * dt)
    den = lam_re * lam_re + lam_im * lam_im
    num_re = a_re - 1.0
    z_re = (num_re * lam_re + a_im * lam_im) / den
    z_im = (a_im * lam_re - num_re * lam_im) / den
    ar_ref[...] = a_re
    ai_ref[...] = a_im
    bbr_ref[...] = z_re * br_ref[...] - z_im * bi_ref[...]
    bbi_ref[...] = z_re * bi_ref[...] + z_im * br_ref[...]


def s5_discretize(lam_re, lam_im, log_step, b_re, b_im):
    two, g, n = lam_re.shape
    p = b_re.shape[-1]
    cols = two * g * n
    flat = lambda a: a.astype(F32).reshape(1, cols)
    ls = jnp.broadcast_to(log_step.astype(F32)[:, :, None], (two, g, n)).reshape(1, cols)
    bt = lambda b: jnp.moveaxis(b.astype(F32), 3, 0).reshape(p, cols)
    tc = _tile(cols, 4096)
    row = pl.BlockSpec((1, tc), lambda i: (0, i))
    mat = pl.BlockSpec((p, tc), lambda i: (0, i))
    a_re, a_im, bb_re, bb_im = pl.pallas_call(
        _s5_discretize_kernel,
        out_shape=[jax.ShapeDtypeStruct((1, cols), F32)] * 2 + [jax.ShapeDtypeStruct((p, cols), F32)] * 2,
        grid=(cols // tc,),
        in_specs=[row, row, row, mat, mat],
        out_specs=[row, row, mat, mat],
        compiler_params=_params("parallel"),
        name="s5_discretize",
    )(flat(lam_re), flat(lam_im), ls, bt(b_re), bt(b_im))
    shp = (two, g, n)
    return a_re.reshape(shp), a_im.reshape(shp), bb_re.reshape((p,) + shp), bb_im.reshape((p,) + shp)


def _s5_scan_kernel(*refs, n_dir_blocks, kc, seg_len, full):
    if full:
        u_ref, bmat_ref, a_ref, ends_ref, cmat_ref, y_ref, state_ref, bu_ref, x_ref = refs
    else:
        u_ref, bmat_ref, a_ref, ends_ref, state_ref, bu_ref = refs
    nseg = S5_SEGMENTS
    half = bu_ref.shape[1] // 2
    dg = pl.program_id(0)
    c = pl.program_id(1)
    rev = dg >= n_dir_blocks
    a_re = a_ref[0:1, :]
    a_im = a_ref[1:2, :]

    @pl.when(c == 0)
    def _():
        if not full:
            state_ref[...] = jnp.zeros(state_ref.shape, F32)
        else:
            pr, pi = a_re, a_im
            for _ in range(int(math.log2(seg_len))):
                pr, pi = pr * pr - pi * pi, 2.0 * pr * pi
            e = ends_ref[...]
            zero = jnp.zeros((1, half), F32)
            fwd = [(zero, zero)]
            for s in range(1, nseg):
                cr, ci = fwd[-1]
                fwd.append((e[s - 1:s, :half] + pr * cr - pi * ci, e[s - 1:s, half:] + pr * ci + pi * cr))
            bwd = [(zero, zero)]
            for s in range(nseg - 2, -1, -1):
                cr, ci = bwd[-1]
                bwd.append((e[s + 1:s + 2, :half] + pr * cr - pi * ci, e[s + 1:s + 2, half:] + pr * ci + pi * cr))
            bwd = bwd[::-1]
            for s in range(nseg):
                state_ref[s:s + 1, :half] = jnp.where(rev, bwd[s][0], fwd[s][0])
                state_ref[s:s + 1, half:] = jnp.where(rev, bwd[s][1], fwd[s][1])

    bu_ref[...] = _dot(u_ref[...].astype(BF16), bmat_ref[...])

    ar = jnp.broadcast_to(a_re, (nseg, half))
    ai = jnp.broadcast_to(a_im, (nseg, half))

    def body(i, carry):
        xr, xi = carry
        k = jnp.where(rev, kc - 1 - i, i)
        rows = pl.ds(pl.multiple_of(k * nseg, nseg), nseg)
        nr = ar * xr - ai * xi + bu_ref[rows, pl.ds(0, half)]
        ni = ar * xi + ai * xr + bu_ref[rows, pl.ds(half, half)]
        if full:
            x_ref[rows, pl.ds(0, half)] = nr
            x_ref[rows, pl.ds(half, half)] = ni
        return nr, ni

    xr, xi = lax.fori_loop(0, kc, body, (state_ref[:, :half], state_ref[:, half:]), unroll=S5_SCAN_UNROLL)
    state_ref[:, :half] = xr
    state_ref[:, half:] = xi

    if full:
        y_ref[...] = _dot(x_ref[...].astype(BF16), cmat_ref[...])
    else:
        @pl.when(c == pl.num_programs(1) - 1)
        def _():
            ends_ref[...] = state_ref[...]


def _s5_scan_call(u, bmat, a2, ends, cmat, *, kc, full):
    seq, d = u.shape
    nseg = S5_SEGMENTS
    seg_len = seq // nseg
    ndg, cw, sw = bmat.shape
    nblk = ndg // 2
    nc = seg_len // kc
    rows = nseg * kc

    def chunk(dg, c):
        return jnp.where(dg >= nblk, nc - 1 - c, c)

    in_specs = [
        pl.BlockSpec((rows, cw), lambda dg, c: (chunk(dg, c), dg % nblk)),
        pl.BlockSpec((None, cw, sw), lambda dg, c: (dg, 0, 0)),
        pl.BlockSpec((None, 2, sw // 2), lambda dg, c: (dg, 0, 0)),
    ]
    ends_spec = pl.BlockSpec((None, nseg, sw), lambda dg, c: (dg, 0, 0))
    scratch = [pltpu.VMEM((nseg, sw), F32), pltpu.VMEM((rows, sw), F32)]
    kern = functools.partial(_s5_scan_kernel, n_dir_blocks=nblk, kc=kc, seg_len=seg_len, full=full)
    if full:
        return pl.pallas_call(
            kern,
            out_shape=jax.ShapeDtypeStruct((2, seq, d), F32),
            grid=(ndg, nc),
            in_specs=in_specs + [ends_spec, pl.BlockSpec((None, sw, cw), lambda dg, c: (dg, 0, 0))],
            out_specs=pl.BlockSpec((None, rows, cw), lambda dg, c: (dg // nblk, chunk(dg, c), dg % nblk)),
            scratch_shapes=scratch + [pltpu.VMEM((rows, sw), F32)],
            compiler_params=_params("parallel", "arbitrary"),
            name="s5_scan",
        )(u, bmat, a2, ends, cmat)
    return pl.pallas_call(
        kern,
        out_shape=jax.ShapeDtypeStruct((ndg, nseg, sw), F32),
        grid=(ndg, nc),
        in_specs=in_specs,
        out_specs=ends_spec,
        scratch_shapes=scratch,
        compiler_params=_params("parallel", "arbitrary"),
        name="s5_segment_ends",
    )(u, bmat, a2)


def _block_diag(m, gps):
    two, g, r, c = m.shape
    m = m.reshape(two, g // gps, gps, r, c)
    eye = jnp.eye(gps, dtype=m.dtype)
    bd = m[:, :, :, :, None, :] * eye[None, None, :, None, :, None]
    return bd.reshape(two * (g // gps), gps * r, gps * c)


def s5_scan(u, lam_re, lam_im, log_step, b_re, b_im, c_re, c_im, *, kc=128):
    seq, d = u.shape
    g = d // S5_GROUP
    gps = min(S5_GROUPS_PER_STEP, g)
    nseg = S5_SEGMENTS
    seg_len = seq // nseg
    assert seq % nseg == 0 and seg_len & (seg_len - 1) == 0
    kc = _tile(seg_len, kc)
    a_re, a_im, bb_re, bb_im = s5_discretize(lam_re, lam_im, log_step, b_re, b_im)
    nblk = g // gps
    ndg = 2 * nblk
    a2 = jnp.stack([a_re.reshape(ndg, gps * S5_STATE), a_im.reshape(ndg, gps * S5_STATE)], axis=1)
    to_gpn = lambda b: jnp.transpose(b, (1, 2, 0, 3))
    bmat = jnp.concatenate([_block_diag(to_gpn(bb_re), gps), _block_diag(to_gpn(bb_im), gps)],
                           axis=2).astype(BF16)
    to_gnp = lambda cc: jnp.swapaxes(cc.astype(F32), 2, 3)
    cmat = jnp.concatenate([_block_diag(to_gnp(c_re), gps), -_block_diag(to_gnp(c_im), gps)],
                           axis=1).astype(BF16)
    ends = _s5_scan_call(u, bmat, a2, None, None, kc=kc, full=False)
    return _s5_scan_call(u, bmat, a2, ends, cmat, kc=kc, full=True)


def _s5_gelu_kernel(y_ref, u_ref, d_ref, o_ref):
    y = y_ref[0] + y_ref[1] + d_ref[...] * u_ref[...]
    o_ref[...] = jax.nn.gelu(y).astype(o_ref.dtype)


def s5_gelu(y2, u, d_skip, *, tm=256):
    seq, d = u.shape
    tm = _tile(seq, tm)
    return pl.pallas_call(
        _s5_gelu_kernel,
        out_shape=jax.ShapeDtypeStruct((seq, d), BF16),
        grid=(seq // tm,),
        in_specs=[pl.BlockSpec((2, tm, d), lambda i: (0, i, 0)), pl.BlockSpec((tm, d), lambda i: (i, 0)),
                  pl.BlockSpec((1, d), lambda i: (0, 0))],
        out_specs=pl.BlockSpec((tm, d), lambda i: (i, 0)),
        compiler_params=_params("parallel"),
        name="s5_gelu",
    )(y2, u, d_skip.reshape(1, d).astype(F32))


def na_mixer(h, hn, w_qkv, w_o, q_gain, k_gain, rpb):
    qkv = matmul(hn, w_qkv.astype(BF16), out_dtype=BF16)
    o = na_attention(qkv, q_gain, k_gain, rpb)
    return matmul_residual(o, w_o.astype(BF16), h, tn=512, tk=w_o.shape[0])


def gqa_mixer(h, hn, w_qkv, w_o, q_gain, k_gain):
    d = h.shape[1]
    n_heads = d // HEAD_DIM
    n_kv = max(1, n_heads // 4)
    qkv = matmul(hn, w_qkv.astype(BF16), out_dtype=BF16)
    gains = jnp.concatenate([jnp.broadcast_to(q_gain.astype(F32), (n_heads, HEAD_DIM)),
                             jnp.broadcast_to(k_gain.astype(F32), (n_kv, HEAD_DIM))]).reshape(-1, 1, HEAD_DIM)
    scales = jnp.concatenate([jnp.full((n_heads, HEAD_DIM), HEAD_DIM ** -0.5 * math.log2(math.e), F32),
                              jnp.ones((n_kv, HEAD_DIM), F32)]).reshape(-1, 1, HEAD_DIM)
    qk = qk_prep(qkv, gains, scales, n_heads + n_kv)
    vt = qkv[:, (n_heads + n_kv) * HEAD_DIM:].T.reshape(n_kv, HEAD_DIM, -1)
    vt = jnp.concatenate([vt, jnp.ones((n_kv, FLASH_ONES_ROWS, vt.shape[2]), BF16)], axis=1)
    o = flash_gqa(qk, vt, n_heads, n_kv)
    return matmul_residual(o, w_o.astype(BF16), h, tn=512, tk=w_o.shape[0])


def _segment_interleave(a):
    seq, d = a.shape
    return a.reshape(S5_SEGMENTS, seq // S5_SEGMENTS, d).transpose(1, 0, 2).reshape(seq, d)


def _segment_deinterleave(a):
    seq, d = a.shape
    return a.reshape(seq // S5_SEGMENTS, S5_SEGMENTS, d).transpose(1, 0, 2).reshape(seq, d)


def s5_mixer(h, hn, w_in, lam_re, lam_im, log_step, b_re, b_im, c_re, c_im, d_skip, w_out):
    u = matmul(_segment_interleave(hn), w_in.astype(BF16), out_dtype=F32)
    y2 = s5_scan(u, lam_re, lam_im, log_step, b_re, b_im, c_re, c_im)
    g = _segment_deinterleave(s5_gelu(y2, u, d_skip))
    return glu_residual(g, w_out.astype(BF16), h)


def kernel(x, p, mix_norm, mlp_norm, ple_norm, na_w_qkv, na_w_o, na_q_gain, na_k_gain, na_rpb, gqa_w_qkv, gqa_w_o, gqa_q_gain, gqa_k_gain, s5_w_in, s5_lam_re, s5_lam_im, s5_log_step, s5_b_re, s5_b_im, s5_c_re, s5_c_im, s5_d, s5_w_out, mlp_w1, mlp_w2, ple_w_proj, ple_w_gate):
    b, seq, d = x.shape
    depth = mix_norm.shape[0]
    outs = []
    for bi in range(b):
        h = x[bi]
        for i in range(depth):
            kind, slot = i % 3, i // 3
            hn = rmsnorm(h, mix_norm[i])
            if kind == 0:
                h = na_mixer(h, hn, na_w_qkv[slot], na_w_o[slot], na_q_gain[slot], na_k_gain[slot], na_rpb[slot])
            elif kind == 1:
                h = gqa_mixer(h, hn, gqa_w_qkv[slot], gqa_w_o[slot], gqa_q_gain[slot], gqa_k_gain[slot])
            else:
                h = s5_mixer(h, hn, s5_w_in[slot], s5_lam_re[slot], s5_lam_im[slot], s5_log_step[slot],
                             s5_b_re[slot], s5_b_im[slot], s5_c_re[slot], s5_c_im[slot], s5_d[slot], s5_w_out[slot])
            hn = rmsnorm(h, mlp_norm[i])
            a = matmul(hn, mlp_w1[i].astype(BF16), out_dtype=BF16, relu2=True)
            h = matmul_residual(a, mlp_w2[i].astype(BF16), h)
            hn = rmsnorm(h, ple_norm[i])
            h = ple_residual(hn, ple_w_gate[i].astype(BF16), p[i, bi].astype(BF16), ple_w_proj[i].astype(BF16), h)
        outs.append(h)
    return jnp.stack(outs)
```

```python
import functools
import math

import jax
import jax.numpy as jnp
from jax import lax
from jax.experimental import pallas as pl
from jax.experimental.pallas import tpu as pltpu

GRID_W = 64
HEAD_DIM = 128
NA_MAX_ROWS = 8
NA_KW = 16
NA_ROWS_PER_BLOCK = 4
NA_WIN_BLOCKS = 3
NA_HEADS_PER_STEP = 4
SOFTMAX_ONES_ROWS = 16
ROPE_THETA = 10000.0
S5_GROUP = 16
S5_STATE = 64
S5_GROUPS_PER_STEP = 16
S5_SEGMENTS = 8
S5_SUBCHUNK = 32
EPS = 1e-6
MASK_VALUE = -1e30
V7X_VMEM_LIMIT_BYTES = 56 * 1024 * 1024

F32 = jnp.float32
BF16 = jnp.bfloat16


def _params(*sem):
    return pltpu.CompilerParams(dimension_semantics=sem, vmem_limit_bytes=V7X_VMEM_LIMIT_BYTES)


def _tile(n, want):
    t = min(n, want)
    while n % t:
        t //= 2
    return t


def _rmsnorm_kernel(x_ref, g_ref, o_ref):
    x = x_ref[...]
    ms = jnp.mean(x * x, axis=-1, keepdims=True)
    o_ref[...] = (x * lax.rsqrt(ms + EPS) * g_ref[...]).astype(o_ref.dtype)


def rmsnorm(x, gain, *, tm=512):
    m, d = x.shape
    tm = _tile(m, tm)
    return pl.pallas_call(
        _rmsnorm_kernel,
        out_shape=jax.ShapeDtypeStruct((m, d), BF16),
        grid=(m // tm,),
        in_specs=[pl.BlockSpec((tm, d), lambda i: (i, 0)), pl.BlockSpec((1, d), lambda i: (0, 0))],
        out_specs=pl.BlockSpec((tm, d), lambda i: (i, 0)),
        compiler_params=_params("parallel"),
        name="rmsnorm",
    )(x, gain.reshape(1, d).astype(F32))


def _dot(a, b):
    return jnp.dot(a, b, preferred_element_type=F32)


def _mm_kernel(a_ref, w_ref, o_ref, *, relu2):
    acc = _dot(a_ref[...], w_ref[...])
    if relu2:
        acc = jnp.square(jnp.maximum(acc, 0.0))
    o_ref[...] = acc.astype(o_ref.dtype)


def matmul(a, w, layer, *, out_dtype, relu2=False, tm=1024, tn=1024):
    m, k = a.shape
    n = w.shape[2]
    tm, tn = _tile(m, tm), _tile(n, tn)
    return pl.pallas_call(
        functools.partial(_mm_kernel, relu2=relu2),
        out_shape=jax.ShapeDtypeStruct((m, n), out_dtype),
        grid=(m // tm, n // tn),
        in_specs=[pl.BlockSpec((tm, k), lambda i, j: (i, 0)),
                  pl.BlockSpec((None, k, tn), lambda i, j: (layer, 0, j))],
        out_specs=pl.BlockSpec((tm, tn), lambda i, j: (i, j)),
        compiler_params=_params("parallel", "parallel"),
        name="matmul",
    )(a, w)


def _mm_res_kernel(a_ref, w_ref, r_ref, o_ref):
    @pl.when(pl.program_id(2) == 0)
    def _():
        o_ref[...] = r_ref[...]

    o_ref[...] += _dot(a_ref[...], w_ref[...])


def matmul_residual(a, w, layer, res, *, tm=1024, tn=1024, tk=2048):
    m, k = a.shape
    n = w.shape[2]
    tm, tn, tk = _tile(m, tm), _tile(n, tn), _tile(k, tk)
    return pl.pallas_call(
        _mm_res_kernel,
        out_shape=jax.ShapeDtypeStruct((m, n), F32),
        grid=(m // tm, n // tn, k // tk),
        in_specs=[
            pl.BlockSpec((tm, tk), lambda i, j, kk: (i, kk)),
            pl.BlockSpec((None, tk, tn), lambda i, j, kk: (layer, kk, j)),
            pl.BlockSpec((tm, tn), lambda i, j, kk: (i, j)),
        ],
        out_specs=pl.BlockSpec((tm, tn), lambda i, j, kk: (i, j)),
        compiler_params=_params("parallel", "parallel", "arbitrary"),
        name="matmul_residual",
    )(a, w, res)


def _glu_res_kernel(a_ref, wa_ref, wg_ref, r_ref, o_ref):
    a = a_ref[...]
    val = _dot(a, wa_ref[...])
    gate = _dot(a, wg_ref[...])
    o_ref[...] = r_ref[...] + val * jax.nn.sigmoid(gate)


def glu_residual(a, w, layer, res, *, tm=512, tn=512):
    m, k = a.shape
    n = w.shape[2] // 2
    tm, tn = _tile(m, tm), _tile(n, tn)
    nj = n // tn
    return pl.pallas_call(
        _glu_res_kernel,
        out_shape=jax.ShapeDtypeStruct((m, n), F32),
        grid=(m // tm, nj),
        in_specs=[
            pl.BlockSpec((tm, k), lambda i, j: (i, 0)),
            pl.BlockSpec((None, k, tn), lambda i, j: (layer, 0, j)),
            pl.BlockSpec((None, k, tn), lambda i, j: (layer, 0, j + nj)),
            pl.BlockSpec((tm, tn), lambda i, j: (i, j)),
        ],
        out_specs=pl.BlockSpec((tm, tn), lambda i, j: (i, j)),
        compiler_params=_params("parallel", "parallel"),
        name="glu_residual",
    )(a, w, w, res)


def _ple_kernel(a_ref, wg_ref, p_ref, wp_ref, r_ref, o_ref):
    gate = jax.nn.sigmoid(_dot(a_ref[...], wg_ref[...]))
    proj = _dot(p_ref[...], wp_ref[...])
    o_ref[...] = r_ref[...] + gate * proj


def ple_residual(a, w_gate, p, w_proj, layer, res, *, tm=1024, tn=512):
    m, k = a.shape
    n = w_gate.shape[2]
    kp = p.shape[2]
    tm, tn = _tile(m, tm), _tile(n, tn)
    return pl.pallas_call(
        _ple_kernel,
        out_shape=jax.ShapeDtypeStruct((m, n), F32),
        grid=(m // tm, n // tn),
        in_specs=[
            pl.BlockSpec((tm, k), lambda i, j: (i, 0)),
            pl.BlockSpec((None, k, tn), lambda i, j: (layer, 0, j)),
            pl.BlockSpec((None, tm, kp), lambda i, j: (layer, i, 0)),
            pl.BlockSpec((None, kp, tn), lambda i, j: (layer, 0, j)),
            pl.BlockSpec((tm, tn), lambda i, j: (i, j)),
        ],
        out_specs=pl.BlockSpec((tm, tn), lambda i, j: (i, j)),
        compiler_params=_params("parallel", "parallel"),
        name="ple_residual",
    )(a, w_gate, p, w_proj, res)


def _head_norm(x, gain):
    x = x.astype(F32)
    return x * lax.rsqrt(jnp.mean(x * x, axis=-1, keepdims=True) + EPS) * gain


def _na_kernel(q_ref, k0_ref, k1_ref, k2_ref, vt0_ref, vt1_ref, vt2_ref, bias_ref, qg_ref, kg_ref, o_ref, *, heads):
    qg, kg = qg_ref[...], kg_ref[...]
    q_scale = HEAD_DIM ** -0.5 * math.log2(math.e)
    ones = jnp.ones((SOFTMAX_ONES_ROWS, NA_WIN_BLOCKS * q_ref.shape[0]), BF16)

    def scores(h):
        sl = slice(h * HEAD_DIM, (h + 1) * HEAD_DIM)
        q = (_head_norm(q_ref[:, sl], qg) * q_scale).astype(BF16)
        k = jnp.concatenate([k0_ref[:, sl], k1_ref[:, sl], k2_ref[:, sl]], axis=0)
        k = _head_norm(k, kg).astype(BF16)
        st = lax.dot_general(k, q, (((1,), (1,)), ((), ())), preferred_element_type=F32)
        return (st + bias_ref[h]).astype(BF16)

    def attend(h, st):
        sl = slice(h * HEAD_DIM, (h + 1) * HEAD_DIM)
        p = jnp.exp2(st - jnp.max(st, axis=0, keepdims=True))
        vt = jnp.concatenate([vt0_ref[sl, :], vt1_ref[sl, :], vt2_ref[sl, :]], axis=1)
        acc = _dot(jnp.concatenate([vt, ones], axis=0), p)
        o_ref[:, sl] = (acc[:HEAD_DIM] / acc[HEAD_DIM:HEAD_DIM + 1]).T.astype(o_ref.dtype)

    st_next = scores(0)
    for h in range(heads):
        st = st_next
        if h + 1 < heads:
            st_next = scores(h + 1)
        attend(h, st)


def _na_bias_table(rpb, rows):
    rb, kh, kw, w = NA_ROWS_PER_BLOCK, NA_MAX_ROWS, NA_KW, GRID_W
    win = rb * NA_WIN_BLOCKS
    assert rows >= win and rows % rb == 0 and rows // rb >= 3
    i = jnp.arange(rb)
    dq = jnp.stack([i, rb + i, 2 * rb + i])
    d0 = jnp.stack([jnp.zeros_like(i), i, jnp.full_like(i, win - kh)])
    a = jnp.arange(win)
    row_ok = (a[None, None, :] >= d0[:, :, None]) & (a[None, None, :] < d0[:, :, None] + kh)
    row_rel = jnp.clip(a[None, None, :] - dq[:, :, None] + (kh - 1), 0, 2 * kh - 2)
    cols = jnp.arange(w)
    c0 = jnp.clip(cols - kw // 2, 0, w - kw)
    col_ok = (cols[None, :] >= c0[:, None]) & (cols[None, :] < c0[:, None] + kw)
    col_rel = jnp.clip(cols[None, :] - cols[:, None] + (kw - 1), 0, 2 * kw - 2)
    ok = row_ok[:, :, None, :, None] & col_ok[None, None, :, None, :]
    row_sel = jax.nn.one_hot(row_rel, 2 * kh - 1, dtype=F32)
    col_sel = jax.nn.one_hot(col_rel, 2 * kw - 1, dtype=F32)
    bias = jnp.einsum("tiar,hrc,wkc->thakiw", row_sel, rpb.astype(F32) * math.log2(math.e), col_sel,
                      precision=lax.Precision.HIGHEST)
    bias = jnp.where(jnp.transpose(ok, (0, 3, 4, 1, 2))[:, None], bias, MASK_VALUE)
    h = rpb.shape[0]
    return bias.reshape(3, h, win * w, rb * w)


def na_attention(qkv, q_gain, k_gain, rpb):
    seq, d3 = qkv.shape
    d = d3 // 3
    nh = d // HEAD_DIM
    rows = seq // GRID_W
    qb = NA_ROWS_PER_BLOCK * GRID_W
    nb = seq // qb
    bias = _na_bias_table(rpb, rows)
    last_start = nb - NA_WIN_BLOCKS
    heads = math.gcd(NA_HEADS_PER_STEP, nh)
    hw = heads * HEAD_DIM
    k_col0 = nh // heads
    vt = qkv[:, 2 * d:].T

    def win_start(b):
        return jnp.clip(b - 1, 0, last_start)

    def bias_map(g, b):
        return ((b > 0).astype(jnp.int32) + (b == nb - 1).astype(jnp.int32), g, 0, 0)

    gain_spec = pl.BlockSpec((1, HEAD_DIM), lambda g, b: (0, 0))
    return pl.pallas_call(
        functools.partial(_na_kernel, heads=heads),
        out_shape=jax.ShapeDtypeStruct((seq, d), BF16),
        grid=(nh // heads, nb),
        in_specs=[pl.BlockSpec((qb, hw), lambda g, b: (b, g))]
        + [pl.BlockSpec((qb, hw), functools.partial(lambda o, g, b: (win_start(b) + o, k_col0 + g), o))
           for o in range(NA_WIN_BLOCKS)]
        + [pl.BlockSpec((hw, qb), functools.partial(lambda o, g, b: (g, win_start(b) + o), o))
           for o in range(NA_WIN_BLOCKS)]
        + [pl.BlockSpec((None, heads, NA_WIN_BLOCKS * qb, qb), bias_map), gain_spec, gain_spec],
        out_specs=pl.BlockSpec((qb, hw), lambda g, b: (b, g)),
        compiler_params=_params("parallel", "arbitrary"),
        name="na_attention",
    )(qkv, qkv, qkv, qkv, vt, vt, vt, bias,
      q_gain.reshape(1, HEAD_DIM).astype(F32), k_gain.reshape(1, HEAD_DIM).astype(F32))


def _rope_tables(seq):
    t = jnp.arange(seq)
    row = (t // GRID_W).astype(F32)
    col = (t % GRID_W).astype(F32)
    nf = HEAD_DIM // 4
    inv = 1.0 / (ROPE_THETA ** (jnp.arange(nf, dtype=F32) / nf))
    ar, ac = row[:, None] * inv, col[:, None] * inv
    cos = jnp.concatenate([jnp.cos(ar), jnp.cos(ar), jnp.cos(ac), jnp.cos(ac)], axis=1)
    sin = jnp.concatenate([-jnp.sin(ar), jnp.sin(ar), -jnp.sin(ac), jnp.sin(ac)], axis=1)
    return cos, sin


def _qk_prep_kernel(x_ref, g_ref, s_ref, cos_ref, sin_ref, o_ref):
    x = _head_norm(x_ref[...], g_ref[...])
    nf = HEAD_DIM // 4
    lane = lax.broadcasted_iota(jnp.int32, x.shape, 1)
    partner = jnp.where((lane % (2 * nf)) < nf,
                        pltpu.roll(x, HEAD_DIM - nf, axis=1), pltpu.roll(x, nf, axis=1))
    y = x * cos_ref[...] + partner * sin_ref[...]
    o_ref[...] = (y * s_ref[...]).astype(o_ref.dtype)


def qk_prep(qkv, gains, scales, n_qk_heads, *, tl=512):
    seq = qkv.shape[0]
    tl = _tile(seq, tl)
    cos, sin = _rope_tables(seq)
    head_spec = pl.BlockSpec((None, 1, HEAD_DIM), lambda i, j: (j, 0, 0))
    tab_spec = pl.BlockSpec((tl, HEAD_DIM), lambda i, j: (i, 0))
    return pl.pallas_call(
        _qk_prep_kernel,
        out_shape=jax.ShapeDtypeStruct((seq, n_qk_heads * HEAD_DIM), BF16),
        grid=(seq // tl, n_qk_heads),
        in_specs=[pl.BlockSpec((tl, HEAD_DIM), lambda i, j: (i, j)), head_spec, head_spec, tab_spec, tab_spec],
        out_specs=pl.BlockSpec((tl, HEAD_DIM), lambda i, j: (i, j)),
        compiler_params=_params("parallel", "arbitrary"),
        name="qk_prep",
    )(qkv, gains, scales, cos, sin)


def _flash_kernel(q_ref, k_ref, vt_ref, o_ref, qs_ref, m_ref, acc_ref, st_ref, *, groups, tq, tk):
    for g in range(groups):
        qs_ref[g * tq:(g + 1) * tq, :] = q_ref[:, g * HEAD_DIM:(g + 1) * HEAD_DIM]
    m_ref[...] = jnp.full(m_ref.shape, -jnp.inf, F32)
    acc_ref[...] = jnp.zeros(acc_ref.shape, F32)
    q = qs_ref[...]
    nk = k_ref.shape[0] // tk

    def scores(c):
        k = k_ref[pl.ds(pl.multiple_of(c * tk, tk), tk), :]
        st = lax.dot_general(k, q, (((1,), (1,)), ((), ())), preferred_element_type=F32)
        return st.astype(BF16)

    def accumulate(st, c):
        vt = vt_ref[:, pl.ds(pl.multiple_of(c * tk, tk), tk)]
        m_old = m_ref[...]
        m_new = jnp.maximum(m_old, jnp.max(st, axis=0, keepdims=True).astype(F32))
        alpha = jnp.exp2(m_old - m_new)
        p = jnp.exp2(st - m_new.astype(BF16))
        acc_ref[...] = alpha * acc_ref[...] + _dot(vt, p)
        m_ref[...] = m_new

    st_ref[0] = scores(0)

    def body(i, carry):
        c = 2 * i
        st_ref[1] = scores(c + 1)
        accumulate(st_ref[0], c)
        st_ref[0] = scores(jnp.minimum(c + 2, nk - 1))
        accumulate(st_ref[1], c + 1)
        return carry

    lax.fori_loop(0, nk // 2, body, 0)
    acc = acc_ref[...]
    out = (acc[:HEAD_DIM] / acc[HEAD_DIM:HEAD_DIM + 1]).T
    for g in range(groups):
        o_ref[:, g * HEAD_DIM:(g + 1) * HEAD_DIM] = out[g * tq:(g + 1) * tq, :].astype(o_ref.dtype)


def flash_gqa(qk, vt, n_heads, n_kv, *, tq=512, tk=512):
    seq = qk.shape[0]
    groups = n_heads // n_kv
    tq, tk = _tile(seq, tq), _tile(seq, tk)
    gw = groups * HEAD_DIM
    vrows = vt.shape[1]
    assert (seq // tk) % 2 == 0
    return pl.pallas_call(
        functools.partial(_flash_kernel, groups=groups, tq=tq, tk=tk),
        out_shape=jax.ShapeDtypeStruct((seq, n_heads * HEAD_DIM), BF16),
        grid=(n_kv, seq // tq),
        in_specs=[
            pl.BlockSpec((tq, gw), lambda g, i: (i, g)),
            pl.BlockSpec((seq, HEAD_DIM), lambda g, i: (0, n_heads + g)),
            pl.BlockSpec((None, vrows, seq), lambda g, i: (g, 0, 0)),
        ],
        out_specs=pl.BlockSpec((tq, gw), lambda g, i: (i, g)),
        scratch_shapes=[
            pltpu.VMEM((groups * tq, HEAD_DIM), BF16),
            pltpu.VMEM((1, groups * tq), F32),
            pltpu.VMEM((vrows, groups * tq), F32),
            pltpu.VMEM((2, tk, groups * tq), BF16),
        ],
        compiler_params=_params("parallel", "arbitrary"),
        name="flash_gqa",
    )(qk, qk, vt)


def _s5_discretize_kernel(lr_ref, li_ref, ls_ref, br_ref, bi_ref, ar_ref, ai_ref, bbr_ref, bbi_ref):
    lam_re, lam_im = lr_ref[...], li_ref[...]
    dt = jnp.exp(ls_ref[...])
    mag = jnp.exp(lam_re * dt)
    a_re = mag * jnp.cos(lam_im * dt)
    a_im = mag * jnp.sin(lam_im * dt)
    den = lam_re * lam_re + lam_im * lam_im
    num_re = a_re - 1.0
    z_re = (num_re * lam_re + a_im * lam_im) / den
    z_im = (a_im * lam_re - num_re * lam_im) / den
    ar_ref[...] = a_re
    ai_ref[...] = a_im
    bbr_ref[...] = z_re * br_ref[...] - z_im * bi_ref[...]
    bbi_ref[...] = z_re * bi_ref[...] + z_im * br_ref[...]


def s5_discretize(lam_re, lam_im, log_step, b_re, b_im):
    two, g, n = lam_re.shape
    p = b_re.shape[-1]
    cols = two * g * n
    flat = lambda a: a.astype(F32).reshape(1, cols)
    ls = jnp.broadcast_to(log_step.astype(F32)[:, :, None], (two, g, n)).reshape(1, cols)
    bt = lambda b: jnp.moveaxis(b.astype(F32), 3, 0).reshape(p, cols)
    tc = _tile(cols, 4096)
    row = pl.BlockSpec((1, tc), lambda i: (0, i))
    mat = pl.BlockSpec((p, tc), lambda i: (0, i))
    a_re, a_im, bb_re, bb_im = pl.pallas_call(
        _s5_discretize_kernel,
        out_shape=[jax.ShapeDtypeStruct((1, cols), F32)] * 2 + [jax.ShapeDtypeStruct((p, cols), F32)] * 2,
        grid=(cols // tc,),
        in_specs=[row, row, row, mat, mat],
        out_specs=[row, row, mat, mat],
        compiler_params=_params("parallel"),
        name="s5_discretize",
    )(flat(lam_re), flat(lam_im), ls, bt(b_re), bt(b_im))
    shp = (two, g, n)
    return a_re.reshape(shp), a_im.reshape(shp), bb_re.reshape((p,) + shp), bb_im.reshape((p,) + shp)


def _s5_scan_kernel(*refs, rev, kc, seg_len, full):
    if full:
        u_ref, bmat_ref, a_ref, ends_ref, cmat_ref, y_ref, state_ref, bu_ref, x_ref = refs
    else:
        u_ref, bmat_ref, a_ref, ends_ref, state_ref, bu_ref = refs
    nseg = S5_SEGMENTS
    half = bmat_ref.shape[1] // 2
    c = pl.program_id(1)
    a_re = a_ref[0:1, :]
    a_im = a_ref[1:2, :]

    @pl.when(c == 0)
    def _():
        if not full:
            state_ref[...] = jnp.zeros(state_ref.shape, F32)
        else:
            pr, pi = a_re, a_im
            for _ in range(int(math.log2(seg_len))):
                pr, pi = pr * pr - pi * pi, 2.0 * pr * pi
            e = ends_ref[...]
            cr = ci = jnp.zeros((1, half), F32)
            order = range(nseg - 1, -1, -1) if rev else range(nseg)
            prev = None
            for s in order:
                if prev is not None:
                    er, ei = e[prev:prev + 1, :half], e[prev:prev + 1, half:]
                    cr, ci = er + pr * cr - pi * ci, ei + pr * ci + pi * cr
                state_ref[s:s + 1, :half] = cr
                state_ref[s:s + 1, half:] = ci
                prev = s

    sub = min(S5_SUBCHUNK, kc)
    nsub = kc // sub
    rs = sub * nseg
    visit = list(range(nsub - 1, -1, -1) if rev else range(nsub))
    ar = jnp.broadcast_to(a_re, (nseg, half))
    ai = jnp.broadcast_to(a_im, (nseg, half))

    def project_in(j, slot):
        bu_ref[slot] = _dot(u_ref[j * rs:(j + 1) * rs, :].astype(BF16), bmat_ref[...])

    def recur(slot, carry):
        xr, xi = carry
        for i in (range(sub - 1, -1, -1) if rev else range(sub)):
            rows = slice(i * nseg, (i + 1) * nseg)
            xr, xi = (ar * xr - ai * xi + bu_ref[slot, rows, :half],
                      ar * xi + ai * xr + bu_ref[slot, rows, half:])
            if full:
                x_ref[slot, rows, :half] = xr
                x_ref[slot, rows, half:] = xi
        return xr, xi

    def project_out(j, slot):
        y_ref[j * rs:(j + 1) * rs, :] = _dot(x_ref[slot].astype(BF16), cmat_ref[...])

    project_in(visit[0], 0)
    carry = (state_ref[:, :half], state_ref[:, half:])
    for n, j in enumerate(visit):
        slot = n % 2
        if n + 1 < nsub:
            project_in(visit[n + 1], 1 - slot)
        carry = recur(slot, carry)
        if full:
            project_out(j, slot)
    state_ref[:, :half] = carry[0]
    state_ref[:, half:] = carry[1]

    if not full:
        @pl.when(c == pl.num_programs(1) - 1)
        def _():
            ends_ref[...] = state_ref[...]


def _s5_scan_call(u, bmat, a2, ends, cmat, *, rev, kc, full):
    seq, d = u.shape
    nseg = S5_SEGMENTS
    seg_len = seq // nseg
    ndg, cw, sw = bmat.shape
    nblk = ndg // 2
    nc = seg_len // kc
    rows = nseg * kc
    sub_rows = nseg * min(S5_SUBCHUNK, kc)
    dg0 = nblk if rev else 0

    def chunk(c):
        return nc - 1 - c if rev else c

    in_specs = [
        pl.BlockSpec((rows, cw), lambda g, c: (chunk(c), g)),
        pl.BlockSpec((None, cw, sw), lambda g, c: (dg0 + g, 0, 0)),
        pl.BlockSpec((None, 2, sw // 2), lambda g, c: (dg0 + g, 0, 0)),
    ]
    ends_spec = pl.BlockSpec((None, nseg, sw), lambda g, c: (g, 0, 0))
    scratch = [pltpu.VMEM((nseg, sw), F32), pltpu.VMEM((2, sub_rows, sw), F32)]
    kern = functools.partial(_s5_scan_kernel, rev=rev, kc=kc, seg_len=seg_len, full=full)
    if full:
        return pl.pallas_call(
            kern,
            out_shape=jax.ShapeDtypeStruct((seq, d), F32),
            grid=(nblk, nc),
            in_specs=in_specs + [ends_spec, pl.BlockSpec((None, sw, cw), lambda g, c: (dg0 + g, 0, 0))],
            out_specs=pl.BlockSpec((rows, cw), lambda g, c: (chunk(c), g)),
            scratch_shapes=scratch + [pltpu.VMEM((2, sub_rows, sw), F32)],
            compiler_params=_params("parallel", "arbitrary"),
            name="s5_scan",
        )(u, bmat, a2, ends, cmat)
    return pl.pallas_call(
        kern,
        out_shape=jax.ShapeDtypeStruct((nblk, nseg, sw), F32),
        grid=(nblk, nc),
        in_specs=in_specs,
        out_specs=ends_spec,
        scratch_shapes=scratch,
        compiler_params=_params("parallel", "arbitrary"),
        name="s5_segment_ends",
    )(u, bmat, a2)


def _block_diag(m, gps):
    two, g, r, c = m.shape
    m = m.reshape(two, g // gps, gps, r, c)
    eye = jnp.eye(gps, dtype=m.dtype)
    bd = m[:, :, :, :, None, :] * eye[None, None, :, None, :, None]
    return bd.reshape(two * (g // gps), gps * r, gps * c)


def s5_scan(u, lam_re, lam_im, log_step, b_re, b_im, c_re, c_im, *, kc=128):
    seq, d = u.shape
    g = d // S5_GROUP
    gps = min(S5_GROUPS_PER_STEP, g)
    nseg = S5_SEGMENTS
    seg_len = seq // nseg
    assert seq % nseg == 0 and seg_len & (seg_len - 1) == 0
    kc = _tile(seg_len, kc)
    a_re, a_im, bb_re, bb_im = s5_discretize(lam_re, lam_im, log_step, b_re, b_im)
    nblk = g // gps
    ndg = 2 * nblk
    a2 = jnp.stack([a_re.reshape(ndg, gps * S5_STATE), a_im.reshape(ndg, gps * S5_STATE)], axis=1)
    to_gpn = lambda b: jnp.transpose(b, (1, 2, 0, 3))
    bmat = jnp.concatenate([_block_diag(to_gpn(bb_re), gps), _block_diag(to_gpn(bb_im), gps)],
                           axis=2).astype(BF16)
    to_gnp = lambda cc: jnp.swapaxes(cc.astype(F32), 2, 3)
    cmat = jnp.concatenate([_block_diag(to_gnp(c_re), gps), -_block_diag(to_gnp(c_im), gps)],
                           axis=1).astype(BF16)
    ys = []
    for rev in (False, True):
        ends = _s5_scan_call(u, bmat, a2, None, None, rev=rev, kc=kc, full=False)
        ys.append(_s5_scan_call(u, bmat, a2, ends, cmat, rev=rev, kc=kc, full=True))
    return ys


def _s5_gelu_kernel(yf_ref, yb_ref, u_ref, d_ref, o_ref):
    y = yf_ref[...] + yb_ref[...] + d_ref[...] * u_ref[...]
    o_ref[...] = jax.nn.gelu(y).astype(o_ref.dtype)


def s5_gelu(y_fwd, y_bwd, u, d_skip, *, tm=256):
    seq, d = u.shape
    tm = _tile(seq, tm)
    row_spec = pl.BlockSpec((tm, d), lambda i: (i, 0))
    return pl.pallas_call(
        _s5_gelu_kernel,
        out_shape=jax.ShapeDtypeStruct((seq, d), BF16),
        grid=(seq // tm,),
        in_specs=[row_spec, row_spec, row_spec, pl.BlockSpec((1, d), lambda i: (0, 0))],
        out_specs=row_spec,
        compiler_params=_params("parallel"),
        name="s5_gelu",
    )(y_fwd, y_bwd, u, d_skip.reshape(1, d).astype(F32))


def na_mixer(h, hn, w_qkv, w_o, slot, q_gain, k_gain, rpb):
    qkv = matmul(hn, w_qkv, slot, out_dtype=BF16)
    o = na_attention(qkv, q_gain, k_gain, rpb)
    return matmul_residual(o, w_o, slot, h, tn=512, tk=w_o.shape[1])


def gqa_mixer(h, hn, w_qkv, w_o, slot, q_gain, k_gain):
    d = h.shape[1]
    n_heads = d // HEAD_DIM
    n_kv = max(1, n_heads // 4)
    qkv = matmul(hn, w_qkv, slot, out_dtype=BF16)
    gains = jnp.concatenate([jnp.broadcast_to(q_gain.astype(F32), (n_heads, HEAD_DIM)),
                             jnp.broadcast_to(k_gain.astype(F32), (n_kv, HEAD_DIM))]).reshape(-1, 1, HEAD_DIM)
    scales = jnp.concatenate([jnp.full((n_heads, HEAD_DIM), HEAD_DIM ** -0.5 * math.log2(math.e), F32),
                              jnp.ones((n_kv, HEAD_DIM), F32)]).reshape(-1, 1, HEAD_DIM)
    qk = qk_prep(qkv, gains, scales, n_heads + n_kv)
    vt = qkv[:, (n_heads + n_kv) * HEAD_DIM:].T.reshape(n_kv, HEAD_DIM, -1)
    vt = jnp.concatenate([vt, jnp.ones((n_kv, SOFTMAX_ONES_ROWS, vt.shape[2]), BF16)], axis=1)
    o = flash_gqa(qk, vt, n_heads, n_kv)
    return matmul_residual(o, w_o, slot, h, tn=512, tk=w_o.shape[1])


def _segment_interleave(a):
    seq, d = a.shape
    return a.reshape(S5_SEGMENTS, seq // S5_SEGMENTS, d).transpose(1, 0, 2).reshape(seq, d)


def _segment_deinterleave(a):
    seq, d = a.shape
    return a.reshape(seq // S5_SEGMENTS, S5_SEGMENTS, d).transpose(1, 0, 2).reshape(seq, d)


def s5_mixer(h, hn, w_in, w_out, slot, lam_re, lam_im, log_step, b_re, b_im, c_re, c_im, d_skip):
    u = matmul(_segment_interleave(hn), w_in, slot, out_dtype=F32)
    y_fwd, y_bwd = s5_scan(u, lam_re, lam_im, log_step, b_re, b_im, c_re, c_im)
    g = _segment_deinterleave(s5_gelu(y_fwd, y_bwd, u, d_skip))
    return glu_residual(g, w_out, slot, h)


def kernel(x, p, mix_norm, mlp_norm, ple_norm, na_w_qkv, na_w_o, na_q_gain, na_k_gain, na_rpb, gqa_w_qkv, gqa_w_o, gqa_q_gain, gqa_k_gain, s5_w_in, s5_lam_re, s5_lam_im, s5_log_step, s5_b_re, s5_b_im, s5_c_re, s5_c_im, s5_d, s5_w_out, mlp_w1, mlp_w2, ple_w_proj, ple_w_gate):
    b, seq, d = x.shape
    depth = mix_norm.shape[0]
    bf = lambda w: w.astype(BF16)
    na_w_qkv, na_w_o, gqa_w_qkv, gqa_w_o = bf(na_w_qkv), bf(na_w_o), bf(gqa_w_qkv), bf(gqa_w_o)
    s5_w_in, s5_w_out = bf(s5_w_in), bf(s5_w_out)
    mlp_w1, mlp_w2, ple_w_proj, ple_w_gate = bf(mlp_w1), bf(mlp_w2), bf(ple_w_proj), bf(ple_w_gate)
    outs = []
    for bi in range(b):
        h = x[bi]
        pb = bf(p[:, bi])
        for i in range(depth):
            kind, slot = i % 3, i // 3
            hn = rmsnorm(h, mix_norm[i])
            if kind == 0:
                h = na_mixer(h, hn, na_w_qkv, na_w_o, slot, na_q_gain[slot], na_k_gain[slot], na_rpb[slot])
            elif kind == 1:
                h = gqa_mixer(h, hn, gqa_w_qkv, gqa_w_o, slot, gqa_q_gain[slot], gqa_k_gain[slot])
            else:
                h = s5_mixer(h, hn, s5_w_in, s5_w_out, slot, s5_lam_re[slot], s5_lam_im[slot], s5_log_step[slot],
                             s5_b_re[slot], s5_b_im[slot], s5_c_re[slot], s5_c_im[slot], s5_d[slot])
            hn = rmsnorm(h, mlp_norm[i])
            a = matmul(hn, mlp_w1, i, out_dtype=BF16, relu2=True)
            h = matmul_residual(a, mlp_w2, i, h)
            hn = rmsnorm(h, ple_norm[i])
            h = ple_residual(hn, ple_w_gate, pb, ple_w_proj, i, h)
        outs.append(h)
    return jnp.stack(outs)
```

```python
import functools
import math

import jax
import jax.numpy as jnp
from jax import lax
from jax.experimental import pallas as pl
from jax.experimental.pallas import tpu as pltpu

GRID_W = 64
HEAD_DIM = 128
NA_MAX_ROWS = 8
NA_KW = 16
NA_ROWS_PER_BLOCK = 4
NA_WIN_BLOCKS = 3
NA_HEADS_PER_STEP = 4
SOFTMAX_ONES_ROWS = 16
ROPE_THETA = 10000.0
S5_GROUP = 16
S5_STATE = 64
S5_GROUPS_PER_STEP = 16
S5_SEGMENTS = 8
S5_SUBCHUNK = 32
EPS = 1e-6
MASK_VALUE = -1e30
V7X_VMEM_LIMIT_BYTES = 56 * 1024 * 1024

F32 = jnp.float32
BF16 = jnp.bfloat16


def _params(*sem):
    return pltpu.CompilerParams(dimension_semantics=sem, vmem_limit_bytes=V7X_VMEM_LIMIT_BYTES)


def _tile(n, want):
    t = min(n, want)
    while n % t:
        t //= 2
    return t


def _rmsnorm_kernel(x_ref, g_ref, o_ref):
    x = x_ref[...]
    ms = jnp.mean(x * x, axis=-1, keepdims=True)
    o_ref[...] = (x * lax.rsqrt(ms + EPS) * g_ref[...]).astype(o_ref.dtype)


def rmsnorm(x, gain, *, tm=512):
    m, d = x.shape
    tm = _tile(m, tm)
    return pl.pallas_call(
        _rmsnorm_kernel,
        out_shape=jax.ShapeDtypeStruct((m, d), BF16),
        grid=(m // tm,),
        in_specs=[pl.BlockSpec((tm, d), lambda i: (i, 0)), pl.BlockSpec((1, d), lambda i: (0, 0))],
        out_specs=pl.BlockSpec((tm, d), lambda i: (i, 0)),
        compiler_params=_params("parallel"),
        name="rmsnorm",
    )(x, gain.reshape(1, d).astype(F32))


def _dot(a, b):
    return jnp.dot(a, b, preferred_element_type=F32)


def _mm_kernel(a_ref, w_ref, o_ref, *, relu2):
    acc = _dot(a_ref[...], w_ref[...])
    if relu2:
        acc = jnp.square(jnp.maximum(acc, 0.0))
    o_ref[...] = acc.astype(o_ref.dtype)


def matmul(a, w, layer, *, out_dtype, relu2=False, tm=1024, tn=1024):
    m, k = a.shape
    n = w.shape[2]
    tm, tn = _tile(m, tm), _tile(n, tn)
    return pl.pallas_call(
        functools.partial(_mm_kernel, relu2=relu2),
        out_shape=jax.ShapeDtypeStruct((m, n), out_dtype),
        grid=(m // tm, n // tn),
        in_specs=[pl.BlockSpec((tm, k), lambda i, j: (i, 0)),
                  pl.BlockSpec((None, k, tn), lambda i, j: (layer, 0, j))],
        out_specs=pl.BlockSpec((tm, tn), lambda i, j: (i, j)),
        compiler_params=_params("parallel", "parallel"),
        name="matmul",
    )(a, w)


def _rope_partner(x):
    nf = HEAD_DIM // 4
    lane = lax.broadcasted_iota(jnp.int32, x.shape, 1)
    return jnp.where((lane % (2 * nf)) < nf, pltpu.roll(x, HEAD_DIM - nf, axis=1), pltpu.roll(x, nf, axis=1))


def _qkv_kernel(*refs, n_norm_tiles, rope):
    if rope:
        a_ref, w_ref, g_ref, cos_ref, sin_ref, o_ref = refs
    else:
        a_ref, w_ref, g_ref, o_ref = refs
    acc = _dot(a_ref[...], w_ref[...])
    is_qk = pl.program_id(1) < n_norm_tiles
    if rope:
        cos = jnp.where(is_qk, cos_ref[...], 1.0)
        sin = jnp.where(is_qk, sin_ref[...], 0.0)
    for h in range(o_ref.shape[1] // HEAD_DIM):
        sl = slice(h * HEAD_DIM, (h + 1) * HEAD_DIM)
        x = acc[:, sl]
        r = lax.rsqrt(jnp.mean(x * x, axis=-1, keepdims=True) + EPS)
        y = x * jnp.where(is_qk, r, 1.0) * g_ref[:, sl]
        if rope:
            y = y * cos + _rope_partner(y) * sin
        o_ref[:, sl] = y.astype(o_ref.dtype)


def qkv_projection(a, w, layer, gain_row, n_qk_cols, rope_tables=None, *, tm=1024, tn=1024):
    m, k = a.shape
    n = w.shape[2]
    tm, tn = _tile(m, tm), math.gcd(_tile(n, tn), n_qk_cols)
    assert tn % HEAD_DIM == 0
    rope = rope_tables is not None
    in_specs = [pl.BlockSpec((tm, k), lambda i, j: (i, 0)),
                pl.BlockSpec((None, k, tn), lambda i, j: (layer, 0, j)),
                pl.BlockSpec((1, tn), lambda i, j: (0, j))]
    args = [a, w, gain_row]
    if rope:
        in_specs += [pl.BlockSpec((tm, HEAD_DIM), lambda i, j: (i, 0))] * 2
        args += list(rope_tables)
    return pl.pallas_call(
        functools.partial(_qkv_kernel, n_norm_tiles=n_qk_cols // tn, rope=rope),
        out_shape=jax.ShapeDtypeStruct((m, n), BF16),
        grid=(m // tm, n // tn),
        in_specs=in_specs,
        out_specs=pl.BlockSpec((tm, tn), lambda i, j: (i, j)),
        compiler_params=_params("parallel", "parallel"),
        name="qkv_projection",
    )(*args)


def _qkv_gain_row(q_gain, k_gain, q_scale, n_q_heads, n_k_heads, n_cols):
    row = jnp.concatenate([jnp.tile(q_gain.astype(F32) * q_scale, n_q_heads), jnp.tile(k_gain.astype(F32), n_k_heads)])
    return jnp.concatenate([row, jnp.ones((n_cols - row.shape[0],), F32)]).reshape(1, n_cols)


def _mm_res_kernel(a_ref, w_ref, r_ref, o_ref):
    @pl.when(pl.program_id(2) == 0)
    def _():
        o_ref[...] = r_ref[...]

    o_ref[...] += _dot(a_ref[...], w_ref[...])


def matmul_residual(a, w, layer, res, *, tm=1024, tn=1024, tk=2048):
    m, k = a.shape
    n = w.shape[2]
    tm, tn, tk = _tile(m, tm), _tile(n, tn), _tile(k, tk)
    return pl.pallas_call(
        _mm_res_kernel,
        out_shape=jax.ShapeDtypeStruct((m, n), F32),
        grid=(m // tm, n // tn, k // tk),
        in_specs=[
            pl.BlockSpec((tm, tk), lambda i, j, kk: (i, kk)),
            pl.BlockSpec((None, tk, tn), lambda i, j, kk: (layer, kk, j)),
            pl.BlockSpec((tm, tn), lambda i, j, kk: (i, j)),
        ],
        out_specs=pl.BlockSpec((tm, tn), lambda i, j, kk: (i, j)),
        compiler_params=_params("parallel", "parallel", "arbitrary"),
        name="matmul_residual",
    )(a, w, res)


def _glu_res_kernel(a_ref, wa_ref, wg_ref, r_ref, o_ref):
    a = a_ref[...]
    val = _dot(a, wa_ref[...])
    gate = _dot(a, wg_ref[...])
    o_ref[...] = r_ref[...] + val * jax.nn.sigmoid(gate)


def glu_residual(a, w, layer, res, *, tm=512, tn=512):
    m, k = a.shape
    n = w.shape[2] // 2
    tm, tn = _tile(m, tm), _tile(n, tn)
    nj = n // tn
    return pl.pallas_call(
        _glu_res_kernel,
        out_shape=jax.ShapeDtypeStruct((m, n), F32),
        grid=(m // tm, nj),
        in_specs=[
            pl.BlockSpec((tm, k), lambda i, j: (i, 0)),
            pl.BlockSpec((None, k, tn), lambda i, j: (layer, 0, j)),
            pl.BlockSpec((None, k, tn), lambda i, j: (layer, 0, j + nj)),
            pl.BlockSpec((tm, tn), lambda i, j: (i, j)),
        ],
        out_specs=pl.BlockSpec((tm, tn), lambda i, j: (i, j)),
        compiler_params=_params("parallel", "parallel"),
        name="glu_residual",
    )(a, w, w, res)


def _ple_kernel(a_ref, wg_ref, p_ref, wp_ref, r_ref, o_ref):
    gate = jax.nn.sigmoid(_dot(a_ref[...], wg_ref[...]))
    proj = _dot(p_ref[...], wp_ref[...])
    o_ref[...] = r_ref[...] + gate * proj


def ple_residual(a, w_gate, p, w_proj, layer, res, *, tm=1024, tn=512):
    m, k = a.shape
    n = w_gate.shape[2]
    kp = p.shape[2]
    tm, tn = _tile(m, tm), _tile(n, tn)
    return pl.pallas_call(
        _ple_kernel,
        out_shape=jax.ShapeDtypeStruct((m, n), F32),
        grid=(m // tm, n // tn),
        in_specs=[
            pl.BlockSpec((tm, k), lambda i, j: (i, 0)),
            pl.BlockSpec((None, k, tn), lambda i, j: (layer, 0, j)),
            pl.BlockSpec((None, tm, kp), lambda i, j: (layer, i, 0)),
            pl.BlockSpec((None, kp, tn), lambda i, j: (layer, 0, j)),
            pl.BlockSpec((tm, tn), lambda i, j: (i, j)),
        ],
        out_specs=pl.BlockSpec((tm, tn), lambda i, j: (i, j)),
        compiler_params=_params("parallel", "parallel"),
        name="ple_residual",
    )(a, w_gate, p, w_proj, res)


def _na_kernel(q_ref, k0_ref, k1_ref, k2_ref, vt0_ref, vt1_ref, vt2_ref, bias_ref, o_ref, *, heads):
    ones = jnp.ones((SOFTMAX_ONES_ROWS, NA_WIN_BLOCKS * q_ref.shape[0]), BF16)

    def scores(h):
        sl = slice(h * HEAD_DIM, (h + 1) * HEAD_DIM)
        k = jnp.concatenate([k0_ref[:, sl], k1_ref[:, sl], k2_ref[:, sl]], axis=0)
        st = lax.dot_general(k, q_ref[:, sl], (((1,), (1,)), ((), ())), preferred_element_type=F32)
        return (st + bias_ref[h]).astype(BF16)

    def attend(h, st):
        sl = slice(h * HEAD_DIM, (h + 1) * HEAD_DIM)
        p = jnp.exp2(st - jnp.max(st, axis=0, keepdims=True))
        vt = jnp.concatenate([vt0_ref[sl, :], vt1_ref[sl, :], vt2_ref[sl, :]], axis=1)
        acc = _dot(jnp.concatenate([vt, ones], axis=0), p)
        o_ref[:, sl] = (acc[:HEAD_DIM] / acc[HEAD_DIM:HEAD_DIM + 1]).T.astype(o_ref.dtype)

    st_next = scores(0)
    for h in range(heads):
        st = st_next
        if h + 1 < heads:
            st_next = scores(h + 1)
        attend(h, st)


def _na_bias_table(rpb, rows):
    rb, kh, kw, w = NA_ROWS_PER_BLOCK, NA_MAX_ROWS, NA_KW, GRID_W
    win = rb * NA_WIN_BLOCKS
    assert rows >= win and rows % rb == 0 and rows // rb >= 3
    i = jnp.arange(rb)
    dq = jnp.stack([i, rb + i, 2 * rb + i])
    d0 = jnp.stack([jnp.zeros_like(i), i, jnp.full_like(i, win - kh)])
    a = jnp.arange(win)
    row_ok = (a[None, None, :] >= d0[:, :, None]) & (a[None, None, :] < d0[:, :, None] + kh)
    row_rel = jnp.clip(a[None, None, :] - dq[:, :, None] + (kh - 1), 0, 2 * kh - 2)
    cols = jnp.arange(w)
    c0 = jnp.clip(cols - kw // 2, 0, w - kw)
    col_ok = (cols[None, :] >= c0[:, None]) & (cols[None, :] < c0[:, None] + kw)
    col_rel = jnp.clip(cols[None, :] - cols[:, None] + (kw - 1), 0, 2 * kw - 2)
    ok = row_ok[:, :, None, :, None] & col_ok[None, None, :, None, :]
    row_sel = jax.nn.one_hot(row_rel, 2 * kh - 1, dtype=F32)
    col_sel = jax.nn.one_hot(col_rel, 2 * kw - 1, dtype=F32)
    bias = jnp.einsum("tiar,hrc,wkc->thakiw", row_sel, rpb.astype(F32) * math.log2(math.e), col_sel,
                      precision=lax.Precision.HIGHEST)
    bias = jnp.where(jnp.transpose(ok, (0, 3, 4, 1, 2))[:, None], bias, MASK_VALUE)
    h = rpb.shape[0]
    return bias.reshape(3, h, win * w, rb * w)


def na_attention(qkv, rpb):
    seq, d3 = qkv.shape
    d = d3 // 3
    nh = d // HEAD_DIM
    rows = seq // GRID_W
    qb = NA_ROWS_PER_BLOCK * GRID_W
    nb = seq // qb
    bias = _na_bias_table(rpb, rows)
    last_start = nb - NA_WIN_BLOCKS
    heads = math.gcd(NA_HEADS_PER_STEP, nh)
    hw = heads * HEAD_DIM
    k_col0 = nh // heads
    vt = qkv[:, 2 * d:].T

    def win_start(b):
        return jnp.clip(b - 1, 0, last_start)

    def bias_map(g, b):
        return ((b > 0).astype(jnp.int32) + (b == nb - 1).astype(jnp.int32), g, 0, 0)

    return pl.pallas_call(
        functools.partial(_na_kernel, heads=heads),
        out_shape=jax.ShapeDtypeStruct((seq, d), BF16),
        grid=(nh // heads, nb),
        in_specs=[pl.BlockSpec((qb, hw), lambda g, b: (b, g))]
        + [pl.BlockSpec((qb, hw), functools.partial(lambda o, g, b: (win_start(b) + o, k_col0 + g), o))
           for o in range(NA_WIN_BLOCKS)]
        + [pl.BlockSpec((hw, qb), functools.partial(lambda o, g, b: (g, win_start(b) + o), o))
           for o in range(NA_WIN_BLOCKS)]
        + [pl.BlockSpec((None, heads, NA_WIN_BLOCKS * qb, qb), bias_map)],
        out_specs=pl.BlockSpec((qb, hw), lambda g, b: (b, g)),
        compiler_params=_params("parallel", "arbitrary"),
        name="na_attention",
    )(qkv, qkv, qkv, qkv, vt, vt, vt, bias)


def _rope_tables(seq):
    t = jnp.arange(seq)
    row = (t // GRID_W).astype(F32)
    col = (t % GRID_W).astype(F32)
    nf = HEAD_DIM // 4
    inv = 1.0 / (ROPE_THETA ** (jnp.arange(nf, dtype=F32) / nf))
    ar, ac = row[:, None] * inv, col[:, None] * inv
    cos = jnp.concatenate([jnp.cos(ar), jnp.cos(ar), jnp.cos(ac), jnp.cos(ac)], axis=1)
    sin = jnp.concatenate([-jnp.sin(ar), jnp.sin(ar), -jnp.sin(ac), jnp.sin(ac)], axis=1)
    return cos, sin


def _flash_kernel(q_ref, k_ref, vt_ref, o_ref, qs_ref, m_ref, acc_ref, st_ref, *, groups, tq, tk):
    for g in range(groups):
        qs_ref[g * tq:(g + 1) * tq, :] = q_ref[:, g * HEAD_DIM:(g + 1) * HEAD_DIM]
    m_ref[...] = jnp.full(m_ref.shape, -jnp.inf, F32)
    acc_ref[...] = jnp.zeros(acc_ref.shape, F32)
    q = qs_ref[...]
    nk = k_ref.shape[0] // tk

    def scores(c):
        k = k_ref[pl.ds(pl.multiple_of(c * tk, tk), tk), :]
        st = lax.dot_general(k, q, (((1,), (1,)), ((), ())), preferred_element_type=F32)
        return st.astype(BF16)

    def accumulate(st, c):
        vt = vt_ref[:, pl.ds(pl.multiple_of(c * tk, tk), tk)]
        m_old = m_ref[...]
        m_new = jnp.maximum(m_old, jnp.max(st, axis=0, keepdims=True).astype(F32))
        alpha = jnp.exp2(m_old - m_new)
        p = jnp.exp2(st - m_new.astype(BF16))
        acc_ref[...] = alpha * acc_ref[...] + _dot(vt, p)
        m_ref[...] = m_new

    st_ref[0] = scores(0)

    def body(i, carry):
        c = 2 * i
        st_ref[1] = scores(c + 1)
        accumulate(st_ref[0], c)
        st_ref[0] = scores(jnp.minimum(c + 2, nk - 1))
        accumulate(st_ref[1], c + 1)
        return carry

    lax.fori_loop(0, nk // 2, body, 0)
    acc = acc_ref[...]
    out = (acc[:HEAD_DIM] / acc[HEAD_DIM:HEAD_DIM + 1]).T
    for g in range(groups):
        o_ref[:, g * HEAD_DIM:(g + 1) * HEAD_DIM] = out[g * tq:(g + 1) * tq, :].astype(o_ref.dtype)


def flash_gqa(qkv, vt, n_heads, n_kv, *, tq=512, tk=512):
    seq = qkv.shape[0]
    groups = n_heads // n_kv
    tq, tk = _tile(seq, tq), _tile(seq, tk)
    gw = groups * HEAD_DIM
    vrows = vt.shape[1]
    assert (seq // tk) % 2 == 0
    return pl.pallas_call(
        functools.partial(_flash_kernel, groups=groups, tq=tq, tk=tk),
        out_shape=jax.ShapeDtypeStruct((seq, n_heads * HEAD_DIM), BF16),
        grid=(n_kv, seq // tq),
        in_specs=[
            pl.BlockSpec((tq, gw), lambda g, i: (i, g)),
            pl.BlockSpec((seq, HEAD_DIM), lambda g, i: (0, n_heads + g)),
            pl.BlockSpec((None, vrows, seq), lambda g, i: (g, 0, 0)),
        ],
        out_specs=pl.BlockSpec((tq, gw), lambda g, i: (i, g)),
        scratch_shapes=[
            pltpu.VMEM((groups * tq, HEAD_DIM), BF16),
            pltpu.VMEM((1, groups * tq), F32),
            pltpu.VMEM((vrows, groups * tq), F32),
            pltpu.VMEM((2, tk, groups * tq), BF16),
        ],
        compiler_params=_params("parallel", "arbitrary"),
        name="flash_gqa",
    )(qkv, qkv, vt)


def _s5_discretize_kernel(lr_ref, li_ref, ls_ref, br_ref, bi_ref, ar_ref, ai_ref, bbr_ref, bbi_ref):
    lam_re, lam_im = lr_ref[...], li_ref[...]
    dt = jnp.exp(ls_ref[...])
    mag = jnp.exp(lam_re * dt)
    a_re = mag * jnp.cos(lam_im * dt)
    a_im = mag * jnp.sin(lam_im * dt)
    den = lam_re * lam_re + lam_im * lam_im
    num_re = a_re - 1.0
    z_re = (num_re * lam_re + a_im * lam_im) / den
    z_im = (a_im * lam_re - num_re * lam_im) / den
    ar_ref[...] = a_re
    ai_ref[...] = a_im
    bbr_ref[...] = z_re * br_ref[...] - z_im * bi_ref[...]
    bbi_ref[...] = z_re * bi_ref[...] + z_im * br_ref[...]


def s5_discretize(lam_re, lam_im, log_step, b_re, b_im):
    two, g, n = lam_re.shape
    p = b_re.shape[-1]
    cols = two * g * n
    flat = lambda a: a.astype(F32).reshape(1, cols)
    ls = jnp.broadcast_to(log_step.astype(F32)[:, :, None], (two, g, n)).reshape(1, cols)
    bt = lambda b: jnp.moveaxis(b.astype(F32), 3, 0).reshape(p, cols)
    tc = _tile(cols, 4096)
    row = pl.BlockSpec((1, tc), lambda i: (0, i))
    mat = pl.BlockSpec((p, tc), lambda i: (0, i))
    a_re, a_im, bb_re, bb_im = pl.pallas_call(
        _s5_discretize_kernel,
        out_shape=[jax.ShapeDtypeStruct((1, cols), F32)] * 2 + [jax.ShapeDtypeStruct((p, cols), F32)] * 2,
        grid=(cols // tc,),
        in_specs=[row, row, row, mat, mat],
        out_specs=[row, row, mat, mat],
        compiler_params=_params("parallel"),
        name="s5_discretize",
    )(flat(lam_re), flat(lam_im), ls, bt(b_re), bt(b_im))
    shp = (two, g, n)
    return a_re.reshape(shp), a_im.reshape(shp), bb_re.reshape((p,) + shp), bb_im.reshape((p,) + shp)


def _s5_scan_kernel(*refs, rev, kc, seg_len, full):
    if full:
        u_ref, bmat_ref, a_ref, ends_ref, cmat_ref, y_ref, state_ref, bu_ref, x_ref = refs
    else:
        u_ref, bmat_ref, a_ref, ends_ref, state_ref, bu_ref = refs
    nseg = S5_SEGMENTS
    half = bmat_ref.shape[1] // 2
    c = pl.program_id(1)
    a_re = a_ref[0:1, :]
    a_im = a_ref[1:2, :]

    @pl.when(c == 0)
    def _():
        if not full:
            state_ref[...] = jnp.zeros(state_ref.shape, F32)
        else:
            pr, pi = a_re, a_im
            for _ in range(int(math.log2(seg_len))):
                pr, pi = pr * pr - pi * pi, 2.0 * pr * pi
            e = ends_ref[...]
            cr = ci = jnp.zeros((1, half), F32)
            order = range(nseg - 1, -1, -1) if rev else range(nseg)
            prev = None
            for s in order:
                if prev is not None:
                    er, ei = e[prev:prev + 1, :half], e[prev:prev + 1, half:]
                    cr, ci = er + pr * cr - pi * ci, ei + pr * ci + pi * cr
                state_ref[s:s + 1, :half] = cr
                state_ref[s:s + 1, half:] = ci
                prev = s

    sub = min(S5_SUBCHUNK, kc)
    nsub = kc // sub
    rs = sub * nseg
    visit = list(range(nsub - 1, -1, -1) if rev else range(nsub))
    ar = jnp.broadcast_to(a_re, (nseg, half))
    ai = jnp.broadcast_to(a_im, (nseg, half))

    def project_in(j, slot):
        bu_ref[slot] = _dot(u_ref[j * rs:(j + 1) * rs, :].astype(BF16), bmat_ref[...])

    def recur(slot, carry):
        xr, xi = carry
        for i in (range(sub - 1, -1, -1) if rev else range(sub)):
            rows = slice(i * nseg, (i + 1) * nseg)
            xr, xi = (ar * xr - ai * xi + bu_ref[slot, rows, :half],
                      ar * xi + ai * xr + bu_ref[slot, rows, half:])
            if full:
                x_ref[slot, rows, :half] = xr
                x_ref[slot, rows, half:] = xi
        return xr, xi

    def project_out(j, slot):
        y_ref[j * rs:(j + 1) * rs, :] = _dot(x_ref[slot].astype(BF16), cmat_ref[...])

    project_in(visit[0], 0)
    carry = (state_ref[:, :half], state_ref[:, half:])
    for n, j in enumerate(visit):
        slot = n % 2
        if n + 1 < nsub:
            project_in(visit[n + 1], 1 - slot)
        carry = recur(slot, carry)
        if full:
            project_out(j, slot)
    state_ref[:, :half] = carry[0]
    state_ref[:, half:] = carry[1]

    if not full:
        @pl.when(c == pl.num_programs(1) - 1)
        def _():
            ends_ref[...] = state_ref[...]


def _s5_scan_call(u, bmat, a2, ends, cmat, *, rev, kc, full):
    seq, d = u.shape
    nseg = S5_SEGMENTS
    seg_len = seq // nseg
    ndg, cw, sw = bmat.shape
    nblk = ndg // 2
    nc = seg_len // kc
    rows = nseg * kc
    sub_rows = nseg * min(S5_SUBCHUNK, kc)
    dg0 = nblk if rev else 0

    def chunk(c):
        return nc - 1 - c if rev else c

    in_specs = [
        pl.BlockSpec((rows, cw), lambda g, c: (chunk(c), g)),
        pl.BlockSpec((None, cw, sw), lambda g, c: (dg0 + g, 0, 0)),
        pl.BlockSpec((None, 2, sw // 2), lambda g, c: (dg0 + g, 0, 0)),
    ]
    ends_spec = pl.BlockSpec((None, nseg, sw), lambda g, c: (g, 0, 0))
    scratch = [pltpu.VMEM((nseg, sw), F32), pltpu.VMEM((2, sub_rows, sw), F32)]
    kern = functools.partial(_s5_scan_kernel, rev=rev, kc=kc, seg_len=seg_len, full=full)
    if full:
        return pl.pallas_call(
            kern,
            out_shape=jax.ShapeDtypeStruct((seq, d), F32),
            grid=(nblk, nc),
            in_specs=in_specs + [ends_spec, pl.BlockSpec((None, sw, cw), lambda g, c: (dg0 + g, 0, 0))],
            out_specs=pl.BlockSpec((rows, cw), lambda g, c: (chunk(c), g)),
            scratch_shapes=scratch + [pltpu.VMEM((2, sub_rows, sw), F32)],
            compiler_params=_params("parallel", "arbitrary"),
            name="s5_scan",
        )(u, bmat, a2, ends, cmat)
    return pl.pallas_call(
        kern,
        out_shape=jax.ShapeDtypeStruct((nblk, nseg, sw), F32),
        grid=(nblk, nc),
        in_specs=in_specs,
        out_specs=ends_spec,
        scratch_shapes=scratch,
        compiler_params=_params("parallel", "arbitrary"),
        name="s5_segment_ends",
    )(u, bmat, a2)


def _block_diag(m, gps):
    two, g, r, c = m.shape
    m = m.reshape(two, g // gps, gps, r, c)
    eye = jnp.eye(gps, dtype=m.dtype)
    bd = m[:, :, :, :, None, :] * eye[None, None, :, None, :, None]
    return bd.reshape(two * (g // gps), gps * r, gps * c)


def s5_scan(u, lam_re, lam_im, log_step, b_re, b_im, c_re, c_im, *, kc=256):
    seq, d = u.shape
    g = d // S5_GROUP
    gps = min(S5_GROUPS_PER_STEP, g)
    nseg = S5_SEGMENTS
    seg_len = seq // nseg
    assert seq % nseg == 0 and seg_len & (seg_len - 1) == 0
    kc = _tile(seg_len, kc)
    a_re, a_im, bb_re, bb_im = s5_discretize(lam_re, lam_im, log_step, b_re, b_im)
    nblk = g // gps
    ndg = 2 * nblk
    a2 = jnp.stack([a_re.reshape(ndg, gps * S5_STATE), a_im.reshape(ndg, gps * S5_STATE)], axis=1)
    to_gpn = lambda b: jnp.transpose(b, (1, 2, 0, 3))
    bmat = jnp.concatenate([_block_diag(to_gpn(bb_re), gps), _block_diag(to_gpn(bb_im), gps)],
                           axis=2).astype(BF16)
    to_gnp = lambda cc: jnp.swapaxes(cc.astype(F32), 2, 3)
    cmat = jnp.concatenate([_block_diag(to_gnp(c_re), gps), -_block_diag(to_gnp(c_im), gps)],
                           axis=1).astype(BF16)
    ys = []
    for rev in (False, True):
        ends = _s5_scan_call(u, bmat, a2, None, None, rev=rev, kc=kc, full=False)
        ys.append(_s5_scan_call(u, bmat, a2, ends, cmat, rev=rev, kc=kc, full=True))
    return ys


def _s5_gelu_kernel(yf_ref, yb_ref, u_ref, d_ref, o_ref):
    y = yf_ref[...] + yb_ref[...] + d_ref[...] * u_ref[...]
    o_ref[...] = jax.nn.gelu(y).astype(o_ref.dtype)


def s5_gelu(y_fwd, y_bwd, u, d_skip, *, tm=256):
    seq, d = u.shape
    tm = _tile(seq, tm)
    row_spec = pl.BlockSpec((tm, d), lambda i: (i, 0))
    return pl.pallas_call(
        _s5_gelu_kernel,
        out_shape=jax.ShapeDtypeStruct((seq, d), BF16),
        grid=(seq // tm,),
        in_specs=[row_spec, row_spec, row_spec, pl.BlockSpec((1, d), lambda i: (0, 0))],
        out_specs=row_spec,
        compiler_params=_params("parallel"),
        name="s5_gelu",
    )(y_fwd, y_bwd, u, d_skip.reshape(1, d).astype(F32))


def na_mixer(h, hn, w_qkv, w_o, slot, q_gain, k_gain, rpb):
    d = h.shape[1]
    n_heads = d // HEAD_DIM
    gain_row = _qkv_gain_row(q_gain, k_gain, HEAD_DIM ** -0.5 * math.log2(math.e), n_heads, n_heads, 3 * d)
    qkv = qkv_projection(hn, w_qkv, slot, gain_row, 2 * d)
    o = na_attention(qkv, rpb)
    return matmul_residual(o, w_o, slot, h, tn=512, tk=w_o.shape[1])


def gqa_mixer(h, hn, w_qkv, w_o, slot, q_gain, k_gain):
    seq, d = h.shape
    n_heads = d // HEAD_DIM
    n_kv = max(1, n_heads // 4)
    n_qk = n_heads + n_kv
    gain_row = _qkv_gain_row(q_gain, k_gain, HEAD_DIM ** -0.5 * math.log2(math.e), n_heads, n_kv, w_qkv.shape[2])
    qkv = qkv_projection(hn, w_qkv, slot, gain_row, n_qk * HEAD_DIM, _rope_tables(seq))
    vt = qkv[:, n_qk * HEAD_DIM:].T.reshape(n_kv, HEAD_DIM, -1)
    vt = jnp.concatenate([vt, jnp.ones((n_kv, SOFTMAX_ONES_ROWS, vt.shape[2]), BF16)], axis=1)
    o = flash_gqa(qkv, vt, n_heads, n_kv)
    return matmul_residual(o, w_o, slot, h, tn=512, tk=w_o.shape[1])


def _segment_interleave(a):
    seq, d = a.shape
    return a.reshape(S5_SEGMENTS, seq // S5_SEGMENTS, d).transpose(1, 0, 2).reshape(seq, d)


def _segment_deinterleave(a):
    seq, d = a.shape
    return a.reshape(seq // S5_SEGMENTS, S5_SEGMENTS, d).transpose(1, 0, 2).reshape(seq, d)


def s5_mixer(h, hn, w_in, w_out, slot, lam_re, lam_im, log_step, b_re, b_im, c_re, c_im, d_skip):
    u = matmul(_segment_interleave(hn), w_in, slot, out_dtype=F32)
    y_fwd, y_bwd = s5_scan(u, lam_re, lam_im, log_step, b_re, b_im, c_re, c_im)
    g = _segment_deinterleave(s5_gelu(y_fwd, y_bwd, u, d_skip))
    return glu_residual(g, w_out, slot, h)


def kernel(x, p, mix_norm, mlp_norm, ple_norm, na_w_qkv, na_w_o, na_q_gain, na_k_gain, na_rpb, gqa_w_qkv, gqa_w_o, gqa_q_gain, gqa_k_gain, s5_w_in, s5_lam_re, s5_lam_im, s5_log_step, s5_b_re, s5_b_im, s5_c_re, s5_c_im, s5_d, s5_w_out, mlp_w1, mlp_w2, ple_w_proj, ple_w_gate):
    b, seq, d = x.shape
    depth = mix_norm.shape[0]
    bf = lambda w: w.astype(BF16)
    na_w_qkv, na_w_o, gqa_w_qkv, gqa_w_o = bf(na_w_qkv), bf(na_w_o), bf(gqa_w_qkv), bf(gqa_w_o)
    s5_w_in, s5_w_out = bf(s5_w_in), bf(s5_w_out)
    mlp_w1, mlp_w2, ple_w_proj, ple_w_gate = bf(mlp_w1), bf(mlp_w2), bf(ple_w_proj), bf(ple_w_gate)
    outs = []
    for bi in range(b):
        h = x[bi]
        pb = bf(p[:, bi])
        for i in range(depth):
            kind, slot = i % 3, i // 3
            hn = rmsnorm(h, mix_norm[i])
            if kind == 0:
                h = na_mixer(h, hn, na_w_qkv, na_w_o, slot, na_q_gain[slot], na_k_gain[slot], na_rpb[slot])
            elif kind == 1:
                h = gqa_mixer(h, hn, gqa_w_qkv, gqa_w_o, slot, gqa_q_gain[slot], gqa_k_gain[slot])
            else:
                h = s5_mixer(h, hn, s5_w_in, s5_w_out, slot, s5_lam_re[slot], s5_lam_im[slot], s5_log_step[slot],
                             s5_b_re[slot], s5_b_im[slot], s5_c_re[slot], s5_c_im[slot], s5_d[slot])
            hn = rmsnorm(h, mlp_norm[i])
            a = matmul(hn, mlp_w1, i, out_dtype=BF16, relu2=True)
            h = matmul_residual(a, mlp_w2, i, h)
            hn = rmsnorm(h, ple_norm[i])
            h = ple_residual(hn, ple_w_gate, pb, ple_w_proj, i, h)
        outs.append(h)
    return jnp.stack(outs)
```

```python
import functools
import math

import jax
import jax.numpy as jnp
from jax import lax
from jax.experimental import pallas as pl
from jax.experimental.pallas import tpu as pltpu

GRID_W = 64
HEAD_DIM = 128
NA_MAX_ROWS = 8
NA_KW = 16
NA_ROWS_PER_BLOCK = 4
NA_WIN_BLOCKS = 3
NA_HEADS_PER_STEP = 4
SOFTMAX_ONES_ROWS = 16
ROPE_THETA = 10000.0
S5_GROUP = 16
S5_STATE = 64
S5_GROUPS_PER_STEP = 16
S5_SEGMENTS = 8
S5_SUBCHUNK = 32
EPS = 1e-6
MASK_VALUE = -1e30
V7X_VMEM_LIMIT_BYTES = 56 * 1024 * 1024

F32 = jnp.float32
BF16 = jnp.bfloat16


def _params(*sem):
    return pltpu.CompilerParams(dimension_semantics=sem, vmem_limit_bytes=V7X_VMEM_LIMIT_BYTES)


def _tile(n, want):
    t = min(n, want)
    while n % t:
        t //= 2
    return t


def _rmsnorm_kernel(x_ref, g_ref, o_ref):
    x = x_ref[...]
    ms = jnp.mean(x * x, axis=-1, keepdims=True)
    o_ref[...] = (x * lax.rsqrt(ms + EPS) * g_ref[...]).astype(o_ref.dtype)


def rmsnorm(x, gain, *, tm=512):
    m, d = x.shape
    tm = _tile(m, tm)
    return pl.pallas_call(
        _rmsnorm_kernel,
        out_shape=jax.ShapeDtypeStruct((m, d), BF16),
        grid=(m // tm,),
        in_specs=[pl.BlockSpec((tm, d), lambda i: (i, 0)), pl.BlockSpec((1, d), lambda i: (0, 0))],
        out_specs=pl.BlockSpec((tm, d), lambda i: (i, 0)),
        compiler_params=_params("parallel"),
        name="rmsnorm",
    )(x, gain.reshape(1, d).astype(F32))


def _dot(a, b):
    return jnp.dot(a, b, preferred_element_type=F32)


def _normed_rows(h_ref, gain_ref, hn_ref):
    @pl.when(pl.program_id(1) == 0)
    def _():
        x = h_ref[...]
        ms = jnp.mean(x * x, axis=-1, keepdims=True)
        hn_ref[...] = (x * lax.rsqrt(ms + EPS) * gain_ref[...]).astype(hn_ref.dtype)

    return hn_ref[...]


def _norm_specs(tm, k):
    return ([pl.BlockSpec((tm, k), lambda i, j: (i, 0)), pl.BlockSpec((1, k), lambda i, j: (0, 0))],
            pltpu.VMEM((tm, k), BF16))


def _mm_kernel(a_ref, w_ref, o_ref, *, relu2):
    acc = _dot(a_ref[...], w_ref[...])
    if relu2:
        acc = jnp.square(jnp.maximum(acc, 0.0))
    o_ref[...] = acc.astype(o_ref.dtype)


def _norm_mm_kernel(h_ref, g_ref, w_ref, o_ref, hn_ref, *, relu2):
    acc = _dot(_normed_rows(h_ref, g_ref, hn_ref), w_ref[...])
    if relu2:
        acc = jnp.square(jnp.maximum(acc, 0.0))
    o_ref[...] = acc.astype(o_ref.dtype)


def norm_matmul(h, gain, w, layer, *, out_dtype, relu2=False, tm=512, tn=1024):
    m, k = h.shape
    n = w.shape[2]
    tm, tn = _tile(m, tm), _tile(n, tn)
    lhs_specs, hn_scratch = _norm_specs(tm, k)
    return pl.pallas_call(
        functools.partial(_norm_mm_kernel, relu2=relu2),
        out_shape=jax.ShapeDtypeStruct((m, n), out_dtype),
        grid=(m // tm, n // tn),
        in_specs=lhs_specs + [pl.BlockSpec((None, k, tn), lambda i, j: (layer, 0, j))],
        out_specs=pl.BlockSpec((tm, tn), lambda i, j: (i, j)),
        scratch_shapes=[hn_scratch],
        compiler_params=_params("parallel", "arbitrary"),
        name="norm_matmul",
    )(h, gain.reshape(1, k).astype(F32), w)


def matmul(a, w, layer, *, out_dtype, relu2=False, tm=1024, tn=1024):
    m, k = a.shape
    n = w.shape[2]
    tm, tn = _tile(m, tm), _tile(n, tn)
    return pl.pallas_call(
        functools.partial(_mm_kernel, relu2=relu2),
        out_shape=jax.ShapeDtypeStruct((m, n), out_dtype),
        grid=(m // tm, n // tn),
        in_specs=[pl.BlockSpec((tm, k), lambda i, j: (i, 0)),
                  pl.BlockSpec((None, k, tn), lambda i, j: (layer, 0, j))],
        out_specs=pl.BlockSpec((tm, tn), lambda i, j: (i, j)),
        compiler_params=_params("parallel", "parallel"),
        name="matmul",
    )(a, w)


def _rope_partner(x):
    nf = HEAD_DIM // 4
    lane = lax.broadcasted_iota(jnp.int32, x.shape, 1)
    return jnp.where((lane % (2 * nf)) < nf, pltpu.roll(x, HEAD_DIM - nf, axis=1), pltpu.roll(x, nf, axis=1))


def _qkv_kernel(*refs, n_norm_tiles, rope):
    if rope:
        h_ref, ng_ref, w_ref, g_ref, cos_ref, sin_ref, o_ref, hn_ref = refs
    else:
        h_ref, ng_ref, w_ref, g_ref, o_ref, hn_ref = refs
    acc = _dot(_normed_rows(h_ref, ng_ref, hn_ref), w_ref[...])
    is_qk = pl.program_id(1) < n_norm_tiles
    if rope:
        cos = jnp.where(is_qk, cos_ref[...], 1.0)
        sin = jnp.where(is_qk, sin_ref[...], 0.0)
    for h in range(o_ref.shape[1] // HEAD_DIM):
        sl = slice(h * HEAD_DIM, (h + 1) * HEAD_DIM)
        x = acc[:, sl]
        r = lax.rsqrt(jnp.mean(x * x, axis=-1, keepdims=True) + EPS)
        y = x * jnp.where(is_qk, r, 1.0) * g_ref[:, sl]
        if rope:
            y = y * cos + _rope_partner(y) * sin
        o_ref[:, sl] = y.astype(o_ref.dtype)


def qkv_projection(h, norm_gain, w, layer, gain_row, n_qk_cols, rope_tables=None, *, tm=512, tn=1024):
    m, k = h.shape
    n = w.shape[2]
    tm, tn = _tile(m, tm), math.gcd(_tile(n, tn), n_qk_cols)
    assert tn % HEAD_DIM == 0
    rope = rope_tables is not None
    lhs_specs, hn_scratch = _norm_specs(tm, k)
    in_specs = lhs_specs + [pl.BlockSpec((None, k, tn), lambda i, j: (layer, 0, j)),
                            pl.BlockSpec((1, tn), lambda i, j: (0, j))]
    args = [h, norm_gain.reshape(1, k).astype(F32), w, gain_row]
    if rope:
        in_specs += [pl.BlockSpec((tm, HEAD_DIM), lambda i, j: (i, 0))] * 2
        args += list(rope_tables)
    return pl.pallas_call(
        functools.partial(_qkv_kernel, n_norm_tiles=n_qk_cols // tn, rope=rope),
        out_shape=jax.ShapeDtypeStruct((m, n), BF16),
        grid=(m // tm, n // tn),
        in_specs=in_specs,
        out_specs=pl.BlockSpec((tm, tn), lambda i, j: (i, j)),
        scratch_shapes=[hn_scratch],
        compiler_params=_params("parallel", "arbitrary"),
        name="qkv_projection",
    )(*args)


def _qkv_gain_row(q_gain, k_gain, q_scale, n_q_heads, n_k_heads, n_cols):
    row = jnp.concatenate([jnp.tile(q_gain.astype(F32) * q_scale, n_q_heads), jnp.tile(k_gain.astype(F32), n_k_heads)])
    return jnp.concatenate([row, jnp.ones((n_cols - row.shape[0],), F32)]).reshape(1, n_cols)


def _mm_res_kernel(a_ref, w_ref, r_ref, o_ref):
    @pl.when(pl.program_id(2) == 0)
    def _():
        o_ref[...] = r_ref[...]

    o_ref[...] += _dot(a_ref[...], w_ref[...])


def matmul_residual(a, w, layer, res, *, tm=1024, tn=1024, tk=2048):
    m, k = a.shape
    n = w.shape[2]
    tm, tn, tk = _tile(m, tm), _tile(n, tn), _tile(k, tk)
    return pl.pallas_call(
        _mm_res_kernel,
        out_shape=jax.ShapeDtypeStruct((m, n), F32),
        grid=(m // tm, n // tn, k // tk),
        in_specs=[
            pl.BlockSpec((tm, tk), lambda i, j, kk: (i, kk)),
            pl.BlockSpec((None, tk, tn), lambda i, j, kk: (layer, kk, j)),
            pl.BlockSpec((tm, tn), lambda i, j, kk: (i, j)),
        ],
        out_specs=pl.BlockSpec((tm, tn), lambda i, j, kk: (i, j)),
        compiler_params=_params("parallel", "parallel", "arbitrary"),
        name="matmul_residual",
    )(a, w, res)


def _glu_res_kernel(a_ref, wa_ref, wg_ref, r_ref, o_ref):
    a = a_ref[...]
    val = _dot(a, wa_ref[...])
    gate = _dot(a, wg_ref[...])
    o_ref[...] = r_ref[...] + val * jax.nn.sigmoid(gate)


def glu_residual(a, w, layer, res, *, tm=512, tn=512):
    m, k = a.shape
    n = w.shape[2] // 2
    tm, tn = _tile(m, tm), _tile(n, tn)
    nj = n // tn
    return pl.pallas_call(
        _glu_res_kernel,
        out_shape=jax.ShapeDtypeStruct((m, n), F32),
        grid=(m // tm, nj),
        in_specs=[
            pl.BlockSpec((tm, k), lambda i, j: (i, 0)),
            pl.BlockSpec((None, k, tn), lambda i, j: (layer, 0, j)),
            pl.BlockSpec((None, k, tn), lambda i, j: (layer, 0, j + nj)),
            pl.BlockSpec((tm, tn), lambda i, j: (i, j)),
        ],
        out_specs=pl.BlockSpec((tm, tn), lambda i, j: (i, j)),
        compiler_params=_params("parallel", "parallel"),
        name="glu_residual",
    )(a, w, w, res)


def _ple_kernel(h_ref, g_ref, wg_ref, p_ref, wp_ref, r_ref, o_ref, hn_ref):
    gate = jax.nn.sigmoid(_dot(_normed_rows(h_ref, g_ref, hn_ref), wg_ref[...]))
    proj = _dot(p_ref[...], wp_ref[...])
    o_ref[...] = r_ref[...] + gate * proj


def ple_residual(h, gain, w_gate, p, w_proj, layer, *, tm=512, tn=512):
    m, k = h.shape
    n = w_gate.shape[2]
    kp = p.shape[2]
    tm, tn = _tile(m, tm), _tile(n, tn)
    lhs_specs, hn_scratch = _norm_specs(tm, k)
    return pl.pallas_call(
        _ple_kernel,
        out_shape=jax.ShapeDtypeStruct((m, n), F32),
        grid=(m // tm, n // tn),
        in_specs=lhs_specs + [
            pl.BlockSpec((None, k, tn), lambda i, j: (layer, 0, j)),
            pl.BlockSpec((None, tm, kp), lambda i, j: (layer, i, 0)),
            pl.BlockSpec((None, kp, tn), lambda i, j: (layer, 0, j)),
            pl.BlockSpec((tm, tn), lambda i, j: (i, j)),
        ],
        out_specs=pl.BlockSpec((tm, tn), lambda i, j: (i, j)),
        scratch_shapes=[hn_scratch],
        compiler_params=_params("parallel", "arbitrary"),
        name="ple_residual",
    )(h, gain.reshape(1, k).astype(F32), w_gate, p, w_proj, h)


def _na_kernel(q_ref, k0_ref, k1_ref, k2_ref, vt0_ref, vt1_ref, vt2_ref, bias_ref, o_ref, *, heads):
    ones = jnp.ones((SOFTMAX_ONES_ROWS, NA_WIN_BLOCKS * q_ref.shape[0]), BF16)

    def scores(h):
        sl = slice(h * HEAD_DIM, (h + 1) * HEAD_DIM)
        k = jnp.concatenate([k0_ref[:, sl], k1_ref[:, sl], k2_ref[:, sl]], axis=0)
        st = lax.dot_general(k, q_ref[:, sl], (((1,), (1,)), ((), ())), preferred_element_type=F32)
        return (st + bias_ref[h]).astype(BF16)

    def attend(h, st):
        sl = slice(h * HEAD_DIM, (h + 1) * HEAD_DIM)
        p = jnp.exp2(st - jnp.max(st, axis=0, keepdims=True))
        vt = jnp.concatenate([vt0_ref[sl, :], vt1_ref[sl, :], vt2_ref[sl, :]], axis=1)
        acc = _dot(jnp.concatenate([vt, ones], axis=0), p)
        o_ref[:, sl] = (acc[:HEAD_DIM] / acc[HEAD_DIM:HEAD_DIM + 1]).T.astype(o_ref.dtype)

    st_next = scores(0)
    for h in range(heads):
        st = st_next
        if h + 1 < heads:
            st_next = scores(h + 1)
        attend(h, st)


def _na_bias_table(rpb, rows):
    rb, kh, kw, w = NA_ROWS_PER_BLOCK, NA_MAX_ROWS, NA_KW, GRID_W
    win = rb * NA_WIN_BLOCKS
    assert rows >= win and rows % rb == 0 and rows // rb >= 3
    i = jnp.arange(rb)
    dq = jnp.stack([i, rb + i, 2 * rb + i])
    d0 = jnp.stack([jnp.zeros_like(i), i, jnp.full_like(i, win - kh)])
    a = jnp.arange(win)
    row_ok = (a[None, None, :] >= d0[:, :, None]) & (a[None, None, :] < d0[:, :, None] + kh)
    row_rel = jnp.clip(a[None, None, :] - dq[:, :, None] + (kh - 1), 0, 2 * kh - 2)
    cols = jnp.arange(w)
    c0 = jnp.clip(cols - kw // 2, 0, w - kw)
    col_ok = (cols[None, :] >= c0[:, None]) & (cols[None, :] < c0[:, None] + kw)
    col_rel = jnp.clip(cols[None, :] - cols[:, None] + (kw - 1), 0, 2 * kw - 2)
    ok = row_ok[:, :, None, :, None] & col_ok[None, None, :, None, :]
    row_sel = jax.nn.one_hot(row_rel, 2 * kh - 1, dtype=F32)
    col_sel = jax.nn.one_hot(col_rel, 2 * kw - 1, dtype=F32)
    bias = jnp.einsum("tiar,hrc,wkc->thakiw", row_sel, rpb.astype(F32) * math.log2(math.e), col_sel,
                      precision=lax.Precision.HIGHEST)
    bias = jnp.where(jnp.transpose(ok, (0, 3, 4, 1, 2))[:, None], bias, MASK_VALUE)
    h = rpb.shape[0]
    return bias.reshape(3, h, win * w, rb * w)


def na_attention(qkv, rpb):
    seq, d3 = qkv.shape
    d = d3 // 3
    nh = d // HEAD_DIM
    rows = seq // GRID_W
    qb = NA_ROWS_PER_BLOCK * GRID_W
    nb = seq // qb
    bias = _na_bias_table(rpb, rows)
    last_start = nb - NA_WIN_BLOCKS
    heads = math.gcd(NA_HEADS_PER_STEP, nh)
    hw = heads * HEAD_DIM
    k_col0 = nh // heads
    vt = qkv[:, 2 * d:].T

    def win_start(b):
        return jnp.clip(b - 1, 0, last_start)

    def bias_map(g, b):
        return ((b > 0).astype(jnp.int32) + (b == nb - 1).astype(jnp.int32), g, 0, 0)

    return pl.pallas_call(
        functools.partial(_na_kernel, heads=heads),
        out_shape=jax.ShapeDtypeStruct((seq, d), BF16),
        grid=(nh // heads, nb),
        in_specs=[pl.BlockSpec((qb, hw), lambda g, b: (b, g))]
        + [pl.BlockSpec((qb, hw), functools.partial(lambda o, g, b: (win_start(b) + o, k_col0 + g), o))
           for o in range(NA_WIN_BLOCKS)]
        + [pl.BlockSpec((hw, qb), functools.partial(lambda o, g, b: (g, win_start(b) + o), o))
           for o in range(NA_WIN_BLOCKS)]
        + [pl.BlockSpec((None, heads, NA_WIN_BLOCKS * qb, qb), bias_map)],
        out_specs=pl.BlockSpec((qb, hw), lambda g, b: (b, g)),
        compiler_params=_params("parallel", "arbitrary"),
        name="na_attention",
    )(qkv, qkv, qkv, qkv, vt, vt, vt, bias)


def _rope_tables(seq):
    t = jnp.arange(seq)
    row = (t // GRID_W).astype(F32)
    col = (t % GRID_W).astype(F32)
    nf = HEAD_DIM // 4
    inv = 1.0 / (ROPE_THETA ** (jnp.arange(nf, dtype=F32) / nf))
    ar, ac = row[:, None] * inv, col[:, None] * inv
    cos = jnp.concatenate([jnp.cos(ar), jnp.cos(ar), jnp.cos(ac), jnp.cos(ac)], axis=1)
    sin = jnp.concatenate([-jnp.sin(ar), jnp.sin(ar), -jnp.sin(ac), jnp.sin(ac)], axis=1)
    return cos, sin


def _flash_kernel(q_ref, k_ref, vt_ref, o_ref, qs_ref, m_ref, acc_ref, st_ref, *, groups, tq, tk):
    for g in range(groups):
        qs_ref[g * tq:(g + 1) * tq, :] = q_ref[:, g * HEAD_DIM:(g + 1) * HEAD_DIM]
    m_ref[...] = jnp.full(m_ref.shape, -jnp.inf, F32)
    acc_ref[...] = jnp.zeros(acc_ref.shape, F32)
    q = qs_ref[...]
    nk = k_ref.shape[0] // tk

    def scores(c):
        k = k_ref[pl.ds(pl.multiple_of(c * tk, tk), tk), :]
        st = lax.dot_general(k, q, (((1,), (1,)), ((), ())), preferred_element_type=F32)
        return st.astype(BF16)

    def accumulate(st, c):
        vt = vt_ref[:, pl.ds(pl.multiple_of(c * tk, tk), tk)]
        m_old = m_ref[...]
        m_new = jnp.maximum(m_old, jnp.max(st, axis=0, keepdims=True).astype(F32))
        alpha = jnp.exp2(m_old - m_new)
        p = jnp.exp2(st - m_new.astype(BF16))
        acc_ref[...] = alpha * acc_ref[...] + _dot(vt, p)
        m_ref[...] = m_new

    st_ref[0] = scores(0)

    def body(i, carry):
        c = 2 * i
        st_ref[1] = scores(c + 1)
        accumulate(st_ref[0], c)
        st_ref[0] = scores(jnp.minimum(c + 2, nk - 1))
        accumulate(st_ref[1], c + 1)
        return carry

    lax.fori_loop(0, nk // 2, body, 0)
    acc = acc_ref[...]
    out = (acc[:HEAD_DIM] / acc[HEAD_DIM:HEAD_DIM + 1]).T
    for g in range(groups):
        o_ref[:, g * HEAD_DIM:(g + 1) * HEAD_DIM] = out[g * tq:(g + 1) * tq, :].astype(o_ref.dtype)


def flash_gqa(qkv, vt, n_heads, n_kv, *, tq=512, tk=512):
    seq = qkv.shape[0]
    groups = n_heads // n_kv
    tq, tk = _tile(seq, tq), _tile(seq, tk)
    gw = groups * HEAD_DIM
    vrows = vt.shape[1]
    assert (seq // tk) % 2 == 0
    return pl.pallas_call(
        functools.partial(_flash_kernel, groups=groups, tq=tq, tk=tk),
        out_shape=jax.ShapeDtypeStruct((seq, n_heads * HEAD_DIM), BF16),
        grid=(n_kv, seq // tq),
        in_specs=[
            pl.BlockSpec((tq, gw), lambda g, i: (i, g)),
            pl.BlockSpec((seq, HEAD_DIM), lambda g, i: (0, n_heads + g)),
            pl.BlockSpec((None, vrows, seq), lambda g, i: (g, 0, 0)),
        ],
        out_specs=pl.BlockSpec((tq, gw), lambda g, i: (i, g)),
        scratch_shapes=[
            pltpu.VMEM((groups * tq, HEAD_DIM), BF16),
            pltpu.VMEM((1, groups * tq), F32),
            pltpu.VMEM((vrows, groups * tq), F32),
            pltpu.VMEM((2, tk, groups * tq), BF16),
        ],
        compiler_params=_params("parallel", "arbitrary"),
        name="flash_gqa",
    )(qkv, qkv, vt)


def _s5_discretize_kernel(lr_ref, li_ref, ls_ref, br_ref, bi_ref, ar_ref, ai_ref, bbr_ref, bbi_ref):
    lam_re, lam_im = lr_ref[...], li_ref[...]
    dt = jnp.exp(ls_ref[...])
    mag = jnp.exp(lam_re * dt)
    a_re = mag * jnp.cos(lam_im * dt)
    a_im = mag * jnp.sin(lam_im * dt)
    den = lam_re * lam_re + lam_im * lam_im
    num_re = a_re - 1.0
    z_re = (num_re * lam_re + a_im * lam_im) / den
    z_im = (a_im * lam_re - num_re * lam_im) / den
    ar_ref[...] = a_re
    ai_ref[...] = a_im
    bbr_ref[...] = z_re * br_ref[...] - z_im * bi_ref[...]
    bbi_ref[...] = z_re * bi_ref[...] + z_im * br_ref[...]


def s5_discretize(lam_re, lam_im, log_step, b_re, b_im):
    two, g, n = lam_re.shape
    p = b_re.shape[-1]
    cols = two * g * n
    flat = lambda a: a.astype(F32).reshape(1, cols)
    ls = jnp.broadcast_to(log_step.astype(F32)[:, :, None], (two, g, n)).reshape(1, cols)
    bt = lambda b: jnp.moveaxis(b.astype(F32), 3, 0).reshape(p, cols)
    tc = _tile(cols, 4096)
    row = pl.BlockSpec((1, tc), lambda i: (0, i))
    mat = pl.BlockSpec((p, tc), lambda i: (0, i))
    a_re, a_im, bb_re, bb_im = pl.pallas_call(
        _s5_discretize_kernel,
        out_shape=[jax.ShapeDtypeStruct((1, cols), F32)] * 2 + [jax.ShapeDtypeStruct((p, cols), F32)] * 2,
        grid=(cols // tc,),
        in_specs=[row, row, row, mat, mat],
        out_specs=[row, row, mat, mat],
        compiler_params=_params("parallel"),
        name="s5_discretize",
    )(flat(lam_re), flat(lam_im), ls, bt(b_re), bt(b_im))
    shp = (two, g, n)
    return a_re.reshape(shp), a_im.reshape(shp), bb_re.reshape((p,) + shp), bb_im.reshape((p,) + shp)


def _s5_scan_kernel(*refs, rev, kc, seg_len, full):
    if full:
        u_ref, bmat_ref, a_ref, ends_ref, cmat_ref, y_ref, state_ref, bu_ref, x_ref = refs
    else:
        u_ref, bmat_ref, a_ref, ends_ref, state_ref, bu_ref = refs
    nseg = S5_SEGMENTS
    half = bmat_ref.shape[1] // 2
    c = pl.program_id(1)
    a_re = a_ref[0:1, :]
    a_im = a_ref[1:2, :]

    @pl.when(c == 0)
    def _():
        if not full:
            state_ref[...] = jnp.zeros(state_ref.shape, F32)
        else:
            pr, pi = a_re, a_im
            for _ in range(int(math.log2(seg_len))):
                pr, pi = pr * pr - pi * pi, 2.0 * pr * pi
            e = ends_ref[...]
            cr = ci = jnp.zeros((1, half), F32)
            order = range(nseg - 1, -1, -1) if rev else range(nseg)
            prev = None
            for s in order:
                if prev is not None:
                    er, ei = e[prev:prev + 1, :half], e[prev:prev + 1, half:]
                    cr, ci = er + pr * cr - pi * ci, ei + pr * ci + pi * cr
                state_ref[s:s + 1, :half] = cr
                state_ref[s:s + 1, half:] = ci
                prev = s

    sub = min(S5_SUBCHUNK, kc)
    nsub = kc // sub
    rs = sub * nseg
    visit = list(range(nsub - 1, -1, -1) if rev else range(nsub))
    ar = jnp.broadcast_to(a_re, (nseg, half))
    ai = jnp.broadcast_to(a_im, (nseg, half))

    def project_in(j, slot):
        bu_ref[slot] = _dot(u_ref[j * rs:(j + 1) * rs, :].astype(BF16), bmat_ref[...])

    def recur(slot, carry):
        xr, xi = carry
        for i in (range(sub - 1, -1, -1) if rev else range(sub)):
            rows = slice(i * nseg, (i + 1) * nseg)
            xr, xi = (ar * xr - ai * xi + bu_ref[slot, rows, :half],
                      ar * xi + ai * xr + bu_ref[slot, rows, half:])
            if full:
                x_ref[slot, rows, :half] = xr
                x_ref[slot, rows, half:] = xi
        return xr, xi

    def project_out(j, slot):
        y_ref[j * rs:(j + 1) * rs, :] = _dot(x_ref[slot].astype(BF16), cmat_ref[...])

    project_in(visit[0], 0)
    carry = (state_ref[:, :half], state_ref[:, half:])
    for n, j in enumerate(visit):
        slot = n % 2
        if n + 1 < nsub:
            project_in(visit[n + 1], 1 - slot)
        carry = recur(slot, carry)
        if full:
            project_out(j, slot)
    state_ref[:, :half] = carry[0]
    state_ref[:, half:] = carry[1]

    if not full:
        @pl.when(c == pl.num_programs(1) - 1)
        def _():
            ends_ref[...] = state_ref[...]


def _s5_scan_call(u, bmat, a2, ends, cmat, *, rev, kc, full):
    seq, d = u.shape
    nseg = S5_SEGMENTS
    seg_len = seq // nseg
    ndg, cw, sw = bmat.shape
    nblk = ndg // 2
    nc = seg_len // kc
    rows = nseg * kc
    sub_rows = nseg * min(S5_SUBCHUNK, kc)
    dg0 = nblk if rev else 0

    def chunk(c):
        return nc - 1 - c if rev else c

    in_specs = [
        pl.BlockSpec((rows, cw), lambda g, c: (chunk(c), g)),
        pl.BlockSpec((None, cw, sw), lambda g, c: (dg0 + g, 0, 0)),
        pl.BlockSpec((None, 2, sw // 2), lambda g, c: (dg0 + g, 0, 0)),
    ]
    ends_spec = pl.BlockSpec((None, nseg, sw), lambda g, c: (g, 0, 0))
    scratch = [pltpu.VMEM((nseg, sw), F32), pltpu.VMEM((2, sub_rows, sw), F32)]
    kern = functools.partial(_s5_scan_kernel, rev=rev, kc=kc, seg_len=seg_len, full=full)
    if full:
        return pl.pallas_call(
            kern,
            out_shape=jax.ShapeDtypeStruct((seq, d), F32),
            grid=(nblk, nc),
            in_specs=in_specs + [ends_spec, pl.BlockSpec((None, sw, cw), lambda g, c: (dg0 + g, 0, 0))],
            out_specs=pl.BlockSpec((rows, cw), lambda g, c: (chunk(c), g)),
            scratch_shapes=scratch + [pltpu.VMEM((2, sub_rows, sw), F32)],
            compiler_params=_params("parallel", "arbitrary"),
            name="s5_scan",
        )(u, bmat, a2, ends, cmat)
    return pl.pallas_call(
        kern,
        out_shape=jax.ShapeDtypeStruct((nblk, nseg, sw), F32),
        grid=(nblk, nc),
        in_specs=in_specs,
        out_specs=ends_spec,
        scratch_shapes=scratch,
        compiler_params=_params("parallel", "arbitrary"),
        name="s5_segment_ends",
    )(u, bmat, a2)


def _block_diag(m, gps):
    two, g, r, c = m.shape
    m = m.reshape(two, g // gps, gps, r, c)
    eye = jnp.eye(gps, dtype=m.dtype)
    bd = m[:, :, :, :, None, :] * eye[None, None, :, None, :, None]
    return bd.reshape(two * (g // gps), gps * r, gps * c)


def s5_scan(u, lam_re, lam_im, log_step, b_re, b_im, c_re, c_im, *, kc=256):
    seq, d = u.shape
    g = d // S5_GROUP
    gps = min(S5_GROUPS_PER_STEP, g)
    nseg = S5_SEGMENTS
    seg_len = seq // nseg
    assert seq % nseg == 0 and seg_len & (seg_len - 1) == 0
    kc = _tile(seg_len, kc)
    a_re, a_im, bb_re, bb_im = s5_discretize(lam_re, lam_im, log_step, b_re, b_im)
    nblk = g // gps
    ndg = 2 * nblk
    a2 = jnp.stack([a_re.reshape(ndg, gps * S5_STATE), a_im.reshape(ndg, gps * S5_STATE)], axis=1)
    to_gpn = lambda b: jnp.transpose(b, (1, 2, 0, 3))
    bmat = jnp.concatenate([_block_diag(to_gpn(bb_re), gps), _block_diag(to_gpn(bb_im), gps)],
                           axis=2).astype(BF16)
    to_gnp = lambda cc: jnp.swapaxes(cc.astype(F32), 2, 3)
    cmat = jnp.concatenate([_block_diag(to_gnp(c_re), gps), -_block_diag(to_gnp(c_im), gps)],
                           axis=1).astype(BF16)
    ys = []
    for rev in (False, True):
        ends = _s5_scan_call(u, bmat, a2, None, None, rev=rev, kc=kc, full=False)
        ys.append(_s5_scan_call(u, bmat, a2, ends, cmat, rev=rev, kc=kc, full=True))
    return ys


def _s5_gelu_kernel(yf_ref, yb_ref, u_ref, d_ref, o_ref):
    y = yf_ref[...] + yb_ref[...] + d_ref[...] * u_ref[...]
    o_ref[...] = jax.nn.gelu(y).astype(o_ref.dtype)


def s5_gelu(y_fwd, y_bwd, u, d_skip, *, tm=256):
    seq, d = u.shape
    tm = _tile(seq, tm)
    row_spec = pl.BlockSpec((tm, d), lambda i: (i, 0))
    return pl.pallas_call(
        _s5_gelu_kernel,
        out_shape=jax.ShapeDtypeStruct((seq, d), BF16),
        grid=(seq // tm,),
        in_specs=[row_spec, row_spec, row_spec, pl.BlockSpec((1, d), lambda i: (0, 0))],
        out_specs=row_spec,
        compiler_params=_params("parallel"),
        name="s5_gelu",
    )(y_fwd, y_bwd, u, d_skip.reshape(1, d).astype(F32))


def na_mixer(h, norm_gain, w_qkv, w_o, slot, q_gain, k_gain, rpb):
    d = h.shape[1]
    n_heads = d // HEAD_DIM
    gain_row = _qkv_gain_row(q_gain, k_gain, HEAD_DIM ** -0.5 * math.log2(math.e), n_heads, n_heads, 3 * d)
    qkv = qkv_projection(h, norm_gain, w_qkv, slot, gain_row, 2 * d)
    o = na_attention(qkv, rpb)
    return matmul_residual(o, w_o, slot, h, tn=512, tk=w_o.shape[1])


def gqa_mixer(h, norm_gain, w_qkv, w_o, slot, q_gain, k_gain):
    seq, d = h.shape
    n_heads = d // HEAD_DIM
    n_kv = max(1, n_heads // 4)
    n_qk = n_heads + n_kv
    gain_row = _qkv_gain_row(q_gain, k_gain, HEAD_DIM ** -0.5 * math.log2(math.e), n_heads, n_kv, w_qkv.shape[2])
    qkv = qkv_projection(h, norm_gain, w_qkv, slot, gain_row, n_qk * HEAD_DIM, _rope_tables(seq))
    vt = qkv[:, n_qk * HEAD_DIM:].T.reshape(n_kv, HEAD_DIM, -1)
    vt = jnp.concatenate([vt, jnp.ones((n_kv, SOFTMAX_ONES_ROWS, vt.shape[2]), BF16)], axis=1)
    o = flash_gqa(qkv, vt, n_heads, n_kv)
    return matmul_residual(o, w_o, slot, h, tn=512, tk=w_o.shape[1])


def _segment_interleave(a):
    seq, d = a.shape
    return a.reshape(S5_SEGMENTS, seq // S5_SEGMENTS, d).transpose(1, 0, 2).reshape(seq, d)


def _segment_deinterleave(a):
    seq, d = a.shape
    return a.reshape(seq // S5_SEGMENTS, S5_SEGMENTS, d).transpose(1, 0, 2).reshape(seq, d)


def s5_mixer(h, hn, w_in, w_out, slot, lam_re, lam_im, log_step, b_re, b_im, c_re, c_im, d_skip):
    u = matmul(_segment_interleave(hn), w_in, slot, out_dtype=F32)
    y_fwd, y_bwd = s5_scan(u, lam_re, lam_im, log_step, b_re, b_im, c_re, c_im)
    g = _segment_deinterleave(s5_gelu(y_fwd, y_bwd, u, d_skip))
    return glu_residual(g, w_out, slot, h)


def kernel(x, p, mix_norm, mlp_norm, ple_norm, na_w_qkv, na_w_o, na_q_gain, na_k_gain, na_rpb, gqa_w_qkv, gqa_w_o, gqa_q_gain, gqa_k_gain, s5_w_in, s5_lam_re, s5_lam_im, s5_log_step, s5_b_re, s5_b_im, s5_c_re, s5_c_im, s5_d, s5_w_out, mlp_w1, mlp_w2, ple_w_proj, ple_w_gate):
    b, seq, d = x.shape
    depth = mix_norm.shape[0]
    bf = lambda w: w.astype(BF16)
    na_w_qkv, na_w_o, gqa_w_qkv, gqa_w_o = bf(na_w_qkv), bf(na_w_o), bf(gqa_w_qkv), bf(gqa_w_o)
    s5_w_in, s5_w_out = bf(s5_w_in), bf(s5_w_out)
    mlp_w1, mlp_w2, ple_w_proj, ple_w_gate = bf(mlp_w1), bf(mlp_w2), bf(ple_w_proj), bf(ple_w_gate)
    outs = []
    for bi in range(b):
        h = x[bi]
        pb = bf(p[:, bi])
        for i in range(depth):
            kind, slot = i % 3, i // 3
            if kind == 0:
                h = na_mixer(h, mix_norm[i], na_w_qkv, na_w_o, slot, na_q_gain[slot], na_k_gain[slot], na_rpb[slot])
            elif kind == 1:
                h = gqa_mixer(h, mix_norm[i], gqa_w_qkv, gqa_w_o, slot, gqa_q_gain[slot], gqa_k_gain[slot])
            else:
                h = s5_mixer(h, rmsnorm(h, mix_norm[i]), s5_w_in, s5_w_out, slot, s5_lam_re[slot], s5_lam_im[slot],
                             s5_log_step[slot], s5_b_re[slot], s5_b_im[slot], s5_c_re[slot], s5_c_im[slot], s5_d[slot])
            a = norm_matmul(h, mlp_norm[i], mlp_w1, i, out_dtype=BF16, relu2=True)
            h = matmul_residual(a, mlp_w2, i, h)
            h = ple_residual(h, ple_norm[i], ple_w_gate, pb, ple_w_proj, i)
        outs.append(h)
    return jnp.stack(outs)
```

```python
import functools
import math

import jax
import jax.numpy as jnp
from jax import lax
from jax.experimental import pallas as pl
from jax.experimental.pallas import tpu as pltpu

GRID_W = 64
HEAD_DIM = 128
NA_MAX_ROWS = 8
NA_KW = 16
NA_ROWS_PER_BLOCK = 4
NA_WIN_BLOCKS = 3
NA_HEADS_PER_STEP = 4
SOFTMAX_ONES_ROWS = 16
FLASH_STAGES_PER_TRIP = 2
ROPE_THETA = 10000.0
S5_GROUP = 16
S5_STATE = 64
S5_GROUPS_PER_STEP = 16
S5_SEGMENTS = 8
S5_SUBCHUNK = 32
EPS = 1e-6
MASK_VALUE = -1e30
V7X_VMEM_LIMIT_BYTES = 56 * 1024 * 1024

F32 = jnp.float32
BF16 = jnp.bfloat16


def _params(*sem):
    return pltpu.CompilerParams(dimension_semantics=sem, vmem_limit_bytes=V7X_VMEM_LIMIT_BYTES)


def _tile(n, want):
    t = min(n, want)
    while n % t:
        t //= 2
    return t


def _rmsnorm_kernel(x_ref, g_ref, o_ref):
    x = x_ref[...]
    ms = jnp.mean(x * x, axis=-1, keepdims=True)
    o_ref[...] = (x * lax.rsqrt(ms + EPS) * g_ref[...]).astype(o_ref.dtype)


def rmsnorm(x, gain, *, tm=512):
    m, d = x.shape
    tm = _tile(m, tm)
    return pl.pallas_call(
        _rmsnorm_kernel,
        out_shape=jax.ShapeDtypeStruct((m, d), BF16),
        grid=(m // tm,),
        in_specs=[pl.BlockSpec((tm, d), lambda i: (i, 0)), pl.BlockSpec((1, d), lambda i: (0, 0))],
        out_specs=pl.BlockSpec((tm, d), lambda i: (i, 0)),
        compiler_params=_params("parallel"),
        name="rmsnorm",
    )(x, gain.reshape(1, d).astype(F32))


def _dot(a, b):
    return jnp.dot(a, b, preferred_element_type=F32)


def _mm_kernel(a_ref, w_ref, o_ref, *, relu2):
    acc = _dot(a_ref[...], w_ref[...])
    if relu2:
        acc = jnp.square(jnp.maximum(acc, 0.0))
    o_ref[...] = acc.astype(o_ref.dtype)


def matmul(a, w, layer, *, out_dtype, relu2=False, tm=1024, tn=1024):
    m, k = a.shape
    n = w.shape[2]
    tm, tn = _tile(m, tm), _tile(n, tn)
    return pl.pallas_call(
        functools.partial(_mm_kernel, relu2=relu2),
        out_shape=jax.ShapeDtypeStruct((m, n), out_dtype),
        grid=(m // tm, n // tn),
        in_specs=[pl.BlockSpec((tm, k), lambda i, j: (i, 0)),
                  pl.BlockSpec((None, k, tn), lambda i, j: (layer, 0, j))],
        out_specs=pl.BlockSpec((tm, tn), lambda i, j: (i, j)),
        compiler_params=_params("parallel", "parallel"),
        name="matmul",
    )(a, w)


def _rope_partner(x):
    nf = HEAD_DIM // 4
    lane = lax.broadcasted_iota(jnp.int32, x.shape, 1)
    return jnp.where((lane % (2 * nf)) < nf, pltpu.roll(x, HEAD_DIM - nf, axis=1), pltpu.roll(x, nf, axis=1))


def _qkv_kernel(*refs, n_norm_tiles, rope):
    if rope:
        a_ref, w_ref, g_ref, cos_ref, sin_ref, o_ref = refs
    else:
        a_ref, w_ref, g_ref, o_ref = refs
    acc = _dot(a_ref[...], w_ref[...])
    is_qk = pl.program_id(1) < n_norm_tiles
    if rope:
        cos = jnp.where(is_qk, cos_ref[...], 1.0)
        sin = jnp.where(is_qk, sin_ref[...], 0.0)
    for h in range(o_ref.shape[1] // HEAD_DIM):
        sl = slice(h * HEAD_DIM, (h + 1) * HEAD_DIM)
        x = acc[:, sl]
        r = lax.rsqrt(jnp.mean(x * x, axis=-1, keepdims=True) + EPS)
        y = x * jnp.where(is_qk, r, 1.0) * g_ref[:, sl]
        if rope:
            y = y * cos + _rope_partner(y) * sin
        o_ref[:, sl] = y.astype(o_ref.dtype)


def qkv_projection(a, w, layer, gain_row, n_qk_cols, rope_tables=None, *, tm=1024, tn=1024):
    m, k = a.shape
    n = w.shape[2]
    tm, tn = _tile(m, tm), math.gcd(_tile(n, tn), n_qk_cols)
    assert tn % HEAD_DIM == 0
    rope = rope_tables is not None
    in_specs = [pl.BlockSpec((tm, k), lambda i, j: (i, 0)),
                pl.BlockSpec((None, k, tn), lambda i, j: (layer, 0, j)),
                pl.BlockSpec((1, tn), lambda i, j: (0, j))]
    args = [a, w, gain_row]
    if rope:
        in_specs += [pl.BlockSpec((tm, HEAD_DIM), lambda i, j: (i, 0))] * 2
        args += list(rope_tables)
    return pl.pallas_call(
        functools.partial(_qkv_kernel, n_norm_tiles=n_qk_cols // tn, rope=rope),
        out_shape=jax.ShapeDtypeStruct((m, n), BF16),
        grid=(m // tm, n // tn),
        in_specs=in_specs,
        out_specs=pl.BlockSpec((tm, tn), lambda i, j: (i, j)),
        compiler_params=_params("parallel", "parallel"),
        name="qkv_projection",
    )(*args)


def _qkv_gain_row(q_gain, k_gain, q_scale, n_q_heads, n_k_heads, n_cols):
    row = jnp.concatenate([jnp.tile(q_gain.astype(F32) * q_scale, n_q_heads), jnp.tile(k_gain.astype(F32), n_k_heads)])
    return jnp.concatenate([row, jnp.ones((n_cols - row.shape[0],), F32)]).reshape(1, n_cols)


def _mm_res_kernel(a_ref, w_ref, r_ref, o_ref):
    @pl.when(pl.program_id(2) == 0)
    def _():
        o_ref[...] = r_ref[...]

    o_ref[...] += _dot(a_ref[...], w_ref[...])


def matmul_residual(a, w, layer, res, *, tm=1024, tn=1024, tk=2048):
    m, k = a.shape
    n = w.shape[2]
    tm, tn, tk = _tile(m, tm), _tile(n, tn), _tile(k, tk)
    return pl.pallas_call(
        _mm_res_kernel,
        out_shape=jax.ShapeDtypeStruct((m, n), F32),
        grid=(m // tm, n // tn, k // tk),
        in_specs=[
            pl.BlockSpec((tm, tk), lambda i, j, kk: (i, kk)),
            pl.BlockSpec((None, tk, tn), lambda i, j, kk: (layer, kk, j)),
            pl.BlockSpec((tm, tn), lambda i, j, kk: (i, j)),
        ],
        out_specs=pl.BlockSpec((tm, tn), lambda i, j, kk: (i, j)),
        compiler_params=_params("parallel", "parallel", "arbitrary"),
        name="matmul_residual",
    )(a, w, res)


def _glu_res_kernel(a_ref, wa_ref, wg_ref, r_ref, o_ref):
    a = a_ref[...]
    val = _dot(a, wa_ref[...])
    gate = _dot(a, wg_ref[...])
    o_ref[...] = r_ref[...] + val * jax.nn.sigmoid(gate)


def glu_residual(a, w, layer, res, *, tm=512, tn=512):
    m, k = a.shape
    n = w.shape[2] // 2
    tm, tn = _tile(m, tm), _tile(n, tn)
    nj = n // tn
    return pl.pallas_call(
        _glu_res_kernel,
        out_shape=jax.ShapeDtypeStruct((m, n), F32),
        grid=(m // tm, nj),
        in_specs=[
            pl.BlockSpec((tm, k), lambda i, j: (i, 0)),
            pl.BlockSpec((None, k, tn), lambda i, j: (layer, 0, j)),
            pl.BlockSpec((None, k, tn), lambda i, j: (layer, 0, j + nj)),
            pl.BlockSpec((tm, tn), lambda i, j: (i, j)),
        ],
        out_specs=pl.BlockSpec((tm, tn), lambda i, j: (i, j)),
        compiler_params=_params("parallel", "parallel"),
        name="glu_residual",
    )(a, w, w, res)


def _ple_kernel(a_ref, wg_ref, p_ref, wp_ref, r_ref, o_ref):
    gate = jax.nn.sigmoid(_dot(a_ref[...], wg_ref[...]))
    proj = _dot(p_ref[...], wp_ref[...])
    o_ref[...] = r_ref[...] + gate * proj


def ple_residual(a, w_gate, p, w_proj, layer, res, *, tm=1024, tn=512):
    m, k = a.shape
    n = w_gate.shape[2]
    kp = p.shape[2]
    tm, tn = _tile(m, tm), _tile(n, tn)
    return pl.pallas_call(
        _ple_kernel,
        out_shape=jax.ShapeDtypeStruct((m, n), F32),
        grid=(m // tm, n // tn),
        in_specs=[
            pl.BlockSpec((tm, k), lambda i, j: (i, 0)),
            pl.BlockSpec((None, k, tn), lambda i, j: (layer, 0, j)),
            pl.BlockSpec((None, tm, kp), lambda i, j: (layer, i, 0)),
            pl.BlockSpec((None, kp, tn), lambda i, j: (layer, 0, j)),
            pl.BlockSpec((tm, tn), lambda i, j: (i, j)),
        ],
        out_specs=pl.BlockSpec((tm, tn), lambda i, j: (i, j)),
        compiler_params=_params("parallel", "parallel"),
        name="ple_residual",
    )(a, w_gate, p, w_proj, res)


def _na_kernel(q_ref, k0_ref, k1_ref, k2_ref, vt0_ref, vt1_ref, vt2_ref, bias_ref, o_ref, *, heads):
    ones = jnp.ones((SOFTMAX_ONES_ROWS, NA_WIN_BLOCKS * q_ref.shape[0]), BF16)

    def scores(h):
        sl = slice(h * HEAD_DIM, (h + 1) * HEAD_DIM)
        k = jnp.concatenate([k0_ref[:, sl], k1_ref[:, sl], k2_ref[:, sl]], axis=0)
        st = lax.dot_general(k, q_ref[:, sl], (((1,), (1,)), ((), ())), preferred_element_type=F32)
        return (st + bias_ref[h]).astype(BF16)

    def attend(h, st):
        sl = slice(h * HEAD_DIM, (h + 1) * HEAD_DIM)
        p = jnp.exp2(st - jnp.max(st, axis=0, keepdims=True))
        vt = jnp.concatenate([vt0_ref[sl, :], vt1_ref[sl, :], vt2_ref[sl, :]], axis=1)
        acc = _dot(jnp.concatenate([vt, ones], axis=0), p)
        o_ref[:, sl] = (acc[:HEAD_DIM] / acc[HEAD_DIM:HEAD_DIM + 1]).T.astype(o_ref.dtype)

    st_next = scores(0)
    for h in range(heads):
        st = st_next
        if h + 1 < heads:
            st_next = scores(h + 1)
        attend(h, st)


def _na_bias_table(rpb, rows):
    rb, kh, kw, w = NA_ROWS_PER_BLOCK, NA_MAX_ROWS, NA_KW, GRID_W
    win = rb * NA_WIN_BLOCKS
    assert rows >= win and rows % rb == 0 and rows // rb >= 3
    i = jnp.arange(rb)
    dq = jnp.stack([i, rb + i, 2 * rb + i])
    d0 = jnp.stack([jnp.zeros_like(i), i, jnp.full_like(i, win - kh)])
    a = jnp.arange(win)
    row_ok = (a[None, None, :] >= d0[:, :, None]) & (a[None, None, :] < d0[:, :, None] + kh)
    row_rel = jnp.clip(a[None, None, :] - dq[:, :, None] + (kh - 1), 0, 2 * kh - 2)
    cols = jnp.arange(w)
    c0 = jnp.clip(cols - kw // 2, 0, w - kw)
    col_ok = (cols[None, :] >= c0[:, None]) & (cols[None, :] < c0[:, None] + kw)
    col_rel = jnp.clip(cols[None, :] - cols[:, None] + (kw - 1), 0, 2 * kw - 2)
    ok = row_ok[:, :, None, :, None] & col_ok[None, None, :, None, :]
    row_sel = jax.nn.one_hot(row_rel, 2 * kh - 1, dtype=F32)
    col_sel = jax.nn.one_hot(col_rel, 2 * kw - 1, dtype=F32)
    bias = jnp.einsum("tiar,hrc,wkc->thakiw", row_sel, rpb.astype(F32) * math.log2(math.e), col_sel,
                      precision=lax.Precision.HIGHEST)
    bias = jnp.where(jnp.transpose(ok, (0, 3, 4, 1, 2))[:, None], bias, MASK_VALUE)
    h = rpb.shape[0]
    return bias.reshape(3, h, win * w, rb * w)


def na_attention(qkv, rpb):
    seq, d3 = qkv.shape
    d = d3 // 3
    nh = d // HEAD_DIM
    rows = seq // GRID_W
    qb = NA_ROWS_PER_BLOCK * GRID_W
    nb = seq // qb
    bias = _na_bias_table(rpb, rows)
    last_start = nb - NA_WIN_BLOCKS
    heads = math.gcd(NA_HEADS_PER_STEP, nh)
    hw = heads * HEAD_DIM
    k_col0 = nh // heads
    vt = qkv[:, 2 * d:].T

    def win_start(b):
        return jnp.clip(b - 1, 0, last_start)

    def bias_map(g, b):
        return ((b > 0).astype(jnp.int32) + (b == nb - 1).astype(jnp.int32), g, 0, 0)

    return pl.pallas_call(
        functools.partial(_na_kernel, heads=heads),
        out_shape=jax.ShapeDtypeStruct((seq, d), BF16),
        grid=(nh // heads, nb),
        in_specs=[pl.BlockSpec((qb, hw), lambda g, b: (b, g))]
        + [pl.BlockSpec((qb, hw), functools.partial(lambda o, g, b: (win_start(b) + o, k_col0 + g), o))
           for o in range(NA_WIN_BLOCKS)]
        + [pl.BlockSpec((hw, qb), functools.partial(lambda o, g, b: (g, win_start(b) + o), o))
           for o in range(NA_WIN_BLOCKS)]
        + [pl.BlockSpec((None, heads, NA_WIN_BLOCKS * qb, qb), bias_map)],
        out_specs=pl.BlockSpec((qb, hw), lambda g, b: (b, g)),
        compiler_params=_params("parallel", "arbitrary"),
        name="na_attention",
    )(qkv, qkv, qkv, qkv, vt, vt, vt, bias)


def _rope_tables(seq):
    t = jnp.arange(seq)
    row = (t // GRID_W).astype(F32)
    col = (t % GRID_W).astype(F32)
    nf = HEAD_DIM // 4
    inv = 1.0 / (ROPE_THETA ** (jnp.arange(nf, dtype=F32) / nf))
    ar, ac = row[:, None] * inv, col[:, None] * inv
    cos = jnp.concatenate([jnp.cos(ar), jnp.cos(ar), jnp.cos(ac), jnp.cos(ac)], axis=1)
    sin = jnp.concatenate([-jnp.sin(ar), jnp.sin(ar), -jnp.sin(ac), jnp.sin(ac)], axis=1)
    return cos, sin


def _flash_kernel(q_ref, k_ref, vt_ref, o_ref, qs_ref, m_ref, acc_ref, st_ref, p_ref, *, groups, tq, tk):
    for g in range(groups):
        qs_ref[g * tq:(g + 1) * tq, :] = q_ref[:, g * HEAD_DIM:(g + 1) * HEAD_DIM]
    m_ref[...] = jnp.full(m_ref.shape, -jnp.inf, F32)
    acc_ref[...] = jnp.zeros(acc_ref.shape, F32)
    q = qs_ref[...]
    nk = k_ref.shape[0] // tk

    def scores(c):
        k = k_ref[pl.ds(pl.multiple_of(c * tk, tk), tk), :]
        st = lax.dot_general(k, q, (((1,), (1,)), ((), ())), preferred_element_type=F32)
        return st.astype(BF16)

    def weighted_values(slot, c):
        vt = vt_ref[:, pl.ds(pl.multiple_of(c * tk, tk), tk)]
        return _dot(vt, p_ref[slot])

    def stage(slot, c):
        st_ref[1 - slot] = scores(jnp.minimum(c + 1, nk - 1))
        st = st_ref[slot]
        m_old = m_ref[...]
        m_new = jnp.maximum(m_old, jnp.max(st, axis=0, keepdims=True).astype(F32))
        alpha = jnp.exp2(m_old - m_new)
        p_ref[slot] = jnp.exp2(st - m_new.astype(BF16))
        m_ref[...] = m_new
        acc_ref[...] = alpha * (acc_ref[...] + weighted_values(1 - slot, jnp.maximum(c - 1, 0)))

    st_ref[0] = scores(0)
    p_ref[1] = jnp.zeros(p_ref.shape[1:], BF16)

    def body(i, carry):
        for s in range(FLASH_STAGES_PER_TRIP):
            stage(s % 2, FLASH_STAGES_PER_TRIP * i + s)
        return carry

    lax.fori_loop(0, nk // FLASH_STAGES_PER_TRIP, body, 0)
    acc = acc_ref[...] + weighted_values(1, nk - 1)
    out = (acc[:HEAD_DIM] / acc[HEAD_DIM:HEAD_DIM + 1]).T
    for g in range(groups):
        o_ref[:, g * HEAD_DIM:(g + 1) * HEAD_DIM] = out[g * tq:(g + 1) * tq, :].astype(o_ref.dtype)


def flash_gqa(qkv, vt, n_heads, n_kv, *, tq=512, tk=512):
    seq = qkv.shape[0]
    groups = n_heads // n_kv
    tq, tk = _tile(seq, tq), _tile(seq, tk)
    gw = groups * HEAD_DIM
    vrows = vt.shape[1]
    assert FLASH_STAGES_PER_TRIP % 2 == 0 and (seq // tk) % FLASH_STAGES_PER_TRIP == 0
    return pl.pallas_call(
        functools.partial(_flash_kernel, groups=groups, tq=tq, tk=tk),
        out_shape=jax.ShapeDtypeStruct((seq, n_heads * HEAD_DIM), BF16),
        grid=(n_kv, seq // tq),
        in_specs=[
            pl.BlockSpec((tq, gw), lambda g, i: (i, g)),
            pl.BlockSpec((seq, HEAD_DIM), lambda g, i: (0, n_heads + g)),
            pl.BlockSpec((None, vrows, seq), lambda g, i: (g, 0, 0)),
        ],
        out_specs=pl.BlockSpec((tq, gw), lambda g, i: (i, g)),
        scratch_shapes=[
            pltpu.VMEM((groups * tq, HEAD_DIM), BF16),
            pltpu.VMEM((1, groups * tq), F32),
            pltpu.VMEM((vrows, groups * tq), F32),
            pltpu.VMEM((2, tk, groups * tq), BF16),
            pltpu.VMEM((2, tk, groups * tq), BF16),
        ],
        compiler_params=_params("parallel", "arbitrary"),
        name="flash_gqa",
    )(qkv, qkv, vt)


def _s5_discretize_kernel(lr_ref, li_ref, ls_ref, br_ref, bi_ref, ar_ref, ai_ref, bbr_ref, bbi_ref):
    lam_re, lam_im = lr_ref[...], li_ref[...]
    dt = jnp.exp(ls_ref[...])
    mag = jnp.exp(lam_re * dt)
    a_re = mag * jnp.cos(lam_im * dt)
    a_im = mag * jnp.sin(lam_im * dt)
    den = lam_re * lam_re + lam_im * lam_im
    num_re = a_re - 1.0
    z_re = (num_re * lam_re + a_im * lam_im) / den
    z_im = (a_im * lam_re - num_re * lam_im) / den
    ar_ref[...] = a_re
    ai_ref[...] = a_im
    bbr_ref[...] = z_re * br_ref[...] - z_im * bi_ref[...]
    bbi_ref[...] = z_re * bi_ref[...] + z_im * br_ref[...]


def s5_discretize(lam_re, lam_im, log_step, b_re, b_im):
    two, g, n = lam_re.shape
    p = b_re.shape[-1]
    cols = two * g * n
    flat = lambda a: a.astype(F32).reshape(1, cols)
    ls = jnp.broadcast_to(log_step.astype(F32)[:, :, None], (two, g, n)).reshape(1, cols)
    bt = lambda b: jnp.moveaxis(b.astype(F32), 3, 0).reshape(p, cols)
    tc = _tile(cols, 4096)
    row = pl.BlockSpec((1, tc), lambda i: (0, i))
    mat = pl.BlockSpec((p, tc), lambda i: (0, i))
    a_re, a_im, bb_re, bb_im = pl.pallas_call(
        _s5_discretize_kernel,
        out_shape=[jax.ShapeDtypeStruct((1, cols), F32)] * 2 + [jax.ShapeDtypeStruct((p, cols), F32)] * 2,
        grid=(cols // tc,),
        in_specs=[row, row, row, mat, mat],
        out_specs=[row, row, mat, mat],
        compiler_params=_params("parallel"),
        name="s5_discretize",
    )(flat(lam_re), flat(lam_im), ls, bt(b_re), bt(b_im))
    shp = (two, g, n)
    return a_re.reshape(shp), a_im.reshape(shp), bb_re.reshape((p,) + shp), bb_im.reshape((p,) + shp)


def _s5_scan_kernel(*refs, rev, kc, seg_len, full):
    if full:
        u_ref, bmat_ref, a_ref, ends_ref, cmat_ref, y_ref, state_ref, bu_ref, x_ref = refs
    else:
        u_ref, bmat_ref, a_ref, ends_ref, state_ref, bu_ref = refs
    nseg = S5_SEGMENTS
    half = bmat_ref.shape[1] // 2
    c = pl.program_id(1)
    a_re = a_ref[0:1, :]
    a_im = a_ref[1:2, :]

    @pl.when(c == 0)
    def _():
        if not full:
            state_ref[...] = jnp.zeros(state_ref.shape, F32)
        else:
            pr, pi = a_re, a_im
            for _ in range(int(math.log2(seg_len))):
                pr, pi = pr * pr - pi * pi, 2.0 * pr * pi
            e = ends_ref[...]
            cr = ci = jnp.zeros((1, half), F32)
            order = range(nseg - 1, -1, -1) if rev else range(nseg)
            prev = None
            for s in order:
                if prev is not None:
                    er, ei = e[prev:prev + 1, :half], e[prev:prev + 1, half:]
                    cr, ci = er + pr * cr - pi * ci, ei + pr * ci + pi * cr
                state_ref[s:s + 1, :half] = cr
                state_ref[s:s + 1, half:] = ci
                prev = s

    sub = min(S5_SUBCHUNK, kc)
    nsub = kc // sub
    rs = sub * nseg
    visit = list(range(nsub - 1, -1, -1) if rev else range(nsub))
    ar = jnp.broadcast_to(a_re, (nseg, half))
    ai = jnp.broadcast_to(a_im, (nseg, half))

    def project_in(j, slot):
        bu_ref[slot] = _dot(u_ref[j * rs:(j + 1) * rs, :].astype(BF16), bmat_ref[...])

    def recur(slot, carry):
        xr, xi = carry
        for i in (range(sub - 1, -1, -1) if rev else range(sub)):
            rows = slice(i * nseg, (i + 1) * nseg)
            xr, xi = (ar * xr - ai * xi + bu_ref[slot, rows, :half],
                      ar * xi + ai * xr + bu_ref[slot, rows, half:])
            if full:
                x_ref[slot, rows, :half] = xr
                x_ref[slot, rows, half:] = xi
        return xr, xi

    def project_out(j, slot):
        y_ref[j * rs:(j + 1) * rs, :] = _dot(x_ref[slot].astype(BF16), cmat_ref[...])

    project_in(visit[0], 0)
    carry = (state_ref[:, :half], state_ref[:, half:])
    for n, j in enumerate(visit):
        slot = n % 2
        if n + 1 < nsub:
            project_in(visit[n + 1], 1 - slot)
        carry = recur(slot, carry)
        if full:
            project_out(j, slot)
    state_ref[:, :half] = carry[0]
    state_ref[:, half:] = carry[1]

    if not full:
        @pl.when(c == pl.num_programs(1) - 1)
        def _():
            ends_ref[...] = state_ref[...]


def _s5_scan_call(u, bmat, a2, ends, cmat, *, rev, kc, full):
    seq, d = u.shape
    nseg = S5_SEGMENTS
    seg_len = seq // nseg
    ndg, cw, sw = bmat.shape
    nblk = ndg // 2
    nc = seg_len // kc
    rows = nseg * kc
    sub_rows = nseg * min(S5_SUBCHUNK, kc)
    dg0 = nblk if rev else 0

    def chunk(c):
        return nc - 1 - c if rev else c

    in_specs = [
        pl.BlockSpec((rows, cw), lambda g, c: (chunk(c), g)),
        pl.BlockSpec((None, cw, sw), lambda g, c: (dg0 + g, 0, 0)),
        pl.BlockSpec((None, 2, sw // 2), lambda g, c: (dg0 + g, 0, 0)),
    ]
    ends_spec = pl.BlockSpec((None, nseg, sw), lambda g, c: (g, 0, 0))
    scratch = [pltpu.VMEM((nseg, sw), F32), pltpu.VMEM((2, sub_rows, sw), F32)]
    kern = functools.partial(_s5_scan_kernel, rev=rev, kc=kc, seg_len=seg_len, full=full)
    if full:
        return pl.pallas_call(
            kern,
            out_shape=jax.ShapeDtypeStruct((seq, d), F32),
            grid=(nblk, nc),
            in_specs=in_specs + [ends_spec, pl.BlockSpec((None, sw, cw), lambda g, c: (dg0 + g, 0, 0))],
            out_specs=pl.BlockSpec((rows, cw), lambda g, c: (chunk(c), g)),
            scratch_shapes=scratch + [pltpu.VMEM((2, sub_rows, sw), F32)],
            compiler_params=_params("parallel", "arbitrary"),
            name="s5_scan",
        )(u, bmat, a2, ends, cmat)
    return pl.pallas_call(
        kern,
        out_shape=jax.ShapeDtypeStruct((nblk, nseg, sw), F32),
        grid=(nblk, nc),
        in_specs=in_specs,
        out_specs=ends_spec,
        scratch_shapes=scratch,
        compiler_params=_params("parallel", "arbitrary"),
        name="s5_segment_ends",
    )(u, bmat, a2)


def _block_diag(m, gps):
    two, g, r, c = m.shape
    m = m.reshape(two, g // gps, gps, r, c)
    eye = jnp.eye(gps, dtype=m.dtype)
    bd = m[:, :, :, :, None, :] * eye[None, None, :, None, :, None]
    return bd.reshape(two * (g // gps), gps * r, gps * c)


def s5_scan(u, lam_re, lam_im, log_step, b_re, b_im, c_re, c_im, *, kc=256):
    seq, d = u.shape
    g = d // S5_GROUP
    gps = min(S5_GROUPS_PER_STEP, g)
    nseg = S5_SEGMENTS
    seg_len = seq // nseg
    assert seq % nseg == 0 and seg_len & (seg_len - 1) == 0
    kc = _tile(seg_len, kc)
    a_re, a_im, bb_re, bb_im = s5_discretize(lam_re, lam_im, log_step, b_re, b_im)
    nblk = g // gps
    ndg = 2 * nblk
    a2 = jnp.stack([a_re.reshape(ndg, gps * S5_STATE), a_im.reshape(ndg, gps * S5_STATE)], axis=1)
    to_gpn = lambda b: jnp.transpose(b, (1, 2, 0, 3))
    bmat = jnp.concatenate([_block_diag(to_gpn(bb_re), gps), _block_diag(to_gpn(bb_im), gps)],
                           axis=2).astype(BF16)
    to_gnp = lambda cc: jnp.swapaxes(cc.astype(F32), 2, 3)
    cmat = jnp.concatenate([_block_diag(to_gnp(c_re), gps), -_block_diag(to_gnp(c_im), gps)],
                           axis=1).astype(BF16)
    ys = []
    for rev in (False, True):
        ends = _s5_scan_call(u, bmat, a2, None, None, rev=rev, kc=kc, full=False)
        ys.append(_s5_scan_call(u, bmat, a2, ends, cmat, rev=rev, kc=kc, full=True))
    return ys


def _s5_gelu_kernel(yf_ref, yb_ref, u_ref, d_ref, o_ref):
    y = yf_ref[...] + yb_ref[...] + d_ref[...] * u_ref[...]
    o_ref[...] = jax.nn.gelu(y).astype(o_ref.dtype)


def s5_gelu(y_fwd, y_bwd, u, d_skip, *, tm=256):
    seq, d = u.shape
    tm = _tile(seq, tm)
    row_spec = pl.BlockSpec((tm, d), lambda i: (i, 0))
    return pl.pallas_call(
        _s5_gelu_kernel,
        out_shape=jax.ShapeDtypeStruct((seq, d), BF16),
        grid=(seq // tm,),
        in_specs=[row_spec, row_spec, row_spec, pl.BlockSpec((1, d), lambda i: (0, 0))],
        out_specs=row_spec,
        compiler_params=_params("parallel"),
        name="s5_gelu",
    )(y_fwd, y_bwd, u, d_skip.reshape(1, d).astype(F32))


def na_mixer(h, hn, w_qkv, w_o, slot, q_gain, k_gain, rpb):
    d = h.shape[1]
    n_heads = d // HEAD_DIM
    gain_row = _qkv_gain_row(q_gain, k_gain, HEAD_DIM ** -0.5 * math.log2(math.e), n_heads, n_heads, 3 * d)
    qkv = qkv_projection(hn, w_qkv, slot, gain_row, 2 * d)
    o = na_attention(qkv, rpb)
    return matmul_residual(o, w_o, slot, h, tn=512, tk=w_o.shape[1])


def gqa_mixer(h, hn, w_qkv, w_o, slot, q_gain, k_gain):
    seq, d = h.shape
    n_heads = d // HEAD_DIM
    n_kv = max(1, n_heads // 4)
    n_qk = n_heads + n_kv
    gain_row = _qkv_gain_row(q_gain, k_gain, HEAD_DIM ** -0.5 * math.log2(math.e), n_heads, n_kv, w_qkv.shape[2])
    qkv = qkv_projection(hn, w_qkv, slot, gain_row, n_qk * HEAD_DIM, _rope_tables(seq))
    vt = qkv[:, n_qk * HEAD_DIM:].T.reshape(n_kv, HEAD_DIM, -1)
    vt = jnp.concatenate([vt, jnp.ones((n_kv, SOFTMAX_ONES_ROWS, vt.shape[2]), BF16)], axis=1)
    o = flash_gqa(qkv, vt, n_heads, n_kv)
    return matmul_residual(o, w_o, slot, h, tn=512, tk=w_o.shape[1])


def _segment_interleave(a):
    seq, d = a.shape
    return a.reshape(S5_SEGMENTS, seq // S5_SEGMENTS, d).transpose(1, 0, 2).reshape(seq, d)


def _segment_deinterleave(a):
    seq, d = a.shape
    return a.reshape(seq // S5_SEGMENTS, S5_SEGMENTS, d).transpose(1, 0, 2).reshape(seq, d)


def s5_mixer(h, hn, w_in, w_out, slot, lam_re, lam_im, log_step, b_re, b_im, c_re, c_im, d_skip):
    u = matmul(_segment_interleave(hn), w_in, slot, out_dtype=F32)
    y_fwd, y_bwd = s5_scan(u, lam_re, lam_im, log_step, b_re, b_im, c_re, c_im)
    g = _segment_deinterleave(s5_gelu(y_fwd, y_bwd, u, d_skip))
    return glu_residual(g, w_out, slot, h)


def kernel(x, p, mix_norm, mlp_norm, ple_norm, na_w_qkv, na_w_o, na_q_gain, na_k_gain, na_rpb, gqa_w_qkv, gqa_w_o, gqa_q_gain, gqa_k_gain, s5_w_in, s5_lam_re, s5_lam_im, s5_log_step, s5_b_re, s5_b_im, s5_c_re, s5_c_im, s5_d, s5_w_out, mlp_w1, mlp_w2, ple_w_proj, ple_w_gate):
    b, seq, d = x.shape
    depth = mix_norm.shape[0]
    bf = lambda w: w.astype(BF16)
    na_w_qkv, na_w_o, gqa_w_qkv, gqa_w_o = bf(na_w_qkv), bf(na_w_o), bf(gqa_w_qkv), bf(gqa_w_o)
    s5_w_in, s5_w_out = bf(s5_w_in), bf(s5_w_out)
    mlp_w1, mlp_w2, ple_w_proj, ple_w_gate = bf(mlp_w1), bf(mlp_w2), bf(ple_w_proj), bf(ple_w_gate)
    outs = []
    for bi in range(b):
        h = x[bi]
        pb = bf(p[:, bi])
        for i in range(depth):
            kind, slot = i % 3, i // 3
            hn = rmsnorm(h, mix_norm[i])
            if kind == 0:
                h = na_mixer(h, hn, na_w_qkv, na_w_o, slot, na_q_gain[slot], na_k_gain[slot], na_rpb[slot])
            elif kind == 1:
                h = gqa_mixer(h, hn, gqa_w_qkv, gqa_w_o, slot, gqa_q_gain[slot], gqa_k_gain[slot])
            else:
                h = s5_mixer(h, hn, s5_w_in, s5_w_out, slot, s5_lam_re[slot], s5_lam_im[slot], s5_log_step[slot],
                             s5_b_re[slot], s5_b_im[slot], s5_c_re[slot], s5_c_im[slot], s5_d[slot])
            hn = rmsnorm(h, mlp_norm[i])
            a = matmul(hn, mlp_w1, i, out_dtype=BF16, relu2=True)
            h = matmul_residual(a, mlp_w2, i, h)
            hn = rmsnorm(h, ple_norm[i])
            h = ple_residual(hn, ple_w_gate, pb, ple_w_proj, i, h)
        outs.append(h)
    return jnp.stack(outs)
```
